```python
import jax, jax.numpy as jnp
from jax import lax
import numpy as np

D_MODEL = 1024
BATCH = 8
SEQ = 4096
DEPTH = 4

D_MIX = D_MODEL
W_GROUP = D_MIX // 4
HEAD_DIM = 64
ROT_DIM = HEAD_DIM // 4
ROPE_THETA = 500000.0
Q_BLOCK = 128
EPS = 1e-6
DSA_HEADS = W_GROUP // HEAD_DIM
IDX_HEADS = 8
IDX_DIM = 32
IDX_ROT = IDX_DIM // 4
DSA_TOPK = 256
NSA_HEADS = W_GROUP // HEAD_DIM
CMP_LEN = 32
CMP_STRIDE = 16
SLC_BLOCK = 64
SLC_TOPN = 16
WINDOW = 512
CONV_WIDTH = 3
POOL_GROUPS = 4
POOL_WINDOWS = (2, 4, 8, 16)
POOL_CH = W_GROUP // POOL_GROUPS

IN_SPLITS = (
    ("a_q", DSA_HEADS * HEAD_DIM), ("a_k", HEAD_DIM), ("a_v", HEAD_DIM),
    ("a_qi", IDX_HEADS * IDX_DIM), ("a_ki", IDX_DIM), ("a_wi", IDX_HEADS), ("a_gate", W_GROUP),
    ("b_q", NSA_HEADS * HEAD_DIM), ("b_kc", HEAD_DIM), ("b_vc", HEAD_DIM), ("b_ks", HEAD_DIM),
    ("b_vs", HEAD_DIM), ("b_kw", HEAD_DIM), ("b_vw", HEAD_DIM), ("b_g", 3 * NSA_HEADS), ("b_gate", W_GROUP),
    ("c_b", W_GROUP), ("c_c", W_GROUP), ("c_x", W_GROUP), ("c_gate", W_GROUP),
    ("d_u", W_GROUP), ("d_gate", W_GROUP),
)
D_IN = sum(w for _, w in IN_SPLITS)

kernel_name = "hybrid_parallel_dsa_nsa_conv_pool"


def split_cols(z):
    offs = np.cumsum([w for _, w in IN_SPLITS])[:-1].tolist()
    return dict(zip([n for n, _ in IN_SPLITS], jnp.split(z, offs, axis=-1)))


def rmsnorm(x, g):
    xf = x.astype(jnp.float32)
    y = xf * lax.rsqrt(jnp.mean(xf * xf, axis=-1, keepdims=True) + EPS)
    return (y * g.astype(jnp.float32)).astype(x.dtype)


def rope(x, pos, rot):
    half = rot // 2
    inv = ROPE_THETA ** (-jnp.arange(half, dtype=jnp.float32) / half)
    ang = pos.astype(jnp.float32)[:, None] * inv[None, :]
    cos = jnp.cos(ang)[:, None, :].astype(x.dtype)
    sin = jnp.sin(ang)[:, None, :].astype(x.dtype)
    x1, x2, rest = x[..., :half], x[..., half:rot], x[..., rot:]
    return jnp.concatenate([x1 * cos - x2 * sin, x2 * cos + x1 * sin, rest], axis=-1)


def masked_softmax(s, valid):
    s = jnp.where(valid, s.astype(jnp.float32), -jnp.inf)
    m = jnp.max(s, axis=-1, keepdims=True)
    m = jnp.where(jnp.isfinite(m), m, 0.0)
    e = jnp.where(valid, jnp.exp(s - m), 0.0)
    return e / jnp.maximum(jnp.sum(e, axis=-1, keepdims=True), 1e-30)


take_rows = jax.vmap(lambda a, i: a[i])


def dsa_mixer(q, k, v, qi, ki, wi):
    B, T = q.shape[:2]
    topk = min(DSA_TOPK, T // 4)
    key_pos = jnp.arange(T)

    def block(i):
        q0 = i * Q_BLOCK
        t = q0 + jnp.arange(Q_BLOCK)
        qb = lax.dynamic_slice_in_dim(q, q0, Q_BLOCK, axis=1)
        qib = lax.dynamic_slice_in_dim(qi, q0, Q_BLOCK, axis=1)
        wib = lax.dynamic_slice_in_dim(wi, q0, Q_BLOCK, axis=1) * (IDX_HEADS ** -0.5)
        logits = jnp.einsum('bqhd,bsd->bqhs', qib, ki) * (IDX_DIM ** -0.5)
        score = jnp.einsum('bqhs,bqh->bqs', jax.nn.relu(logits), wib).astype(jnp.float32)
        causal = key_pos[None, :] <= t[:, None]
        score = jnp.where(causal[None], score, -jnp.inf)
        _, idx = lax.top_k(score, topk)
        k_sel = take_rows(k, idx)
        v_sel = take_rows(v, idx)
        s = jnp.einsum('bqhd,bqkd->bhqk', qb, k_sel) * (HEAD_DIM ** -0.5)
        valid = (idx <= t[None, :, None])[:, None]
        p = masked_softmax(s, valid).astype(v.dtype)
        return jnp.einsum('bhqk,bqkd->bqhd', p, v_sel)

    out = lax.map(block, jnp.arange(T // Q_BLOCK))
    return jnp.moveaxis(out, 0, 1).reshape(B, T, -1)


def nsa_mixer(q, kc_tok, vc_tok, ks, vs, kw, vw, gates, pe_cmp, w_cmp_k, w_cmp_v):
    B, T = q.shape[:2]
    dtype = q.dtype
    n_cmp = (T - CMP_LEN) // CMP_STRIDE + 1
    n_slc = T // SLC_BLOCK
    topn = min(SLC_TOPN, n_slc)
    scale = HEAD_DIM ** -0.5
    cmp_start = jnp.arange(n_cmp) * CMP_STRIDE
    cmp_end = cmp_start + CMP_LEN - 1
    cmp_idx = cmp_start[:, None] + jnp.arange(CMP_LEN)[None, :]

    def compress(tok, w):
        blocks = tok[:, cmp_idx] + pe_cmp
        return blocks.reshape(B, n_cmp, CMP_LEN * HEAD_DIM) @ w

    k_cmp = rope(compress(kc_tok, w_cmp_k)[:, :, None], cmp_end, ROT_DIM)[:, :, 0]
    v_cmp = compress(vc_tok, w_cmp_v)
    slc_start = jnp.arange(n_slc) * SLC_BLOCK
    overlap = ((cmp_start[:, None] < slc_start[None, :] + SLC_BLOCK)
               & (cmp_end[:, None] >= slc_start[None, :])).astype(jnp.float32)
    ks_blocks = ks.reshape(B, n_slc, SLC_BLOCK, HEAD_DIM)
    vs_blocks = vs.reshape(B, n_slc, SLC_BLOCK, HEAD_DIM)
    kw_pad = jnp.pad(kw, ((0, 0), (WINDOW, 0), (0, 0)))
    vw_pad = jnp.pad(vw, ((0, 0), (WINDOW, 0), (0, 0)))
    blk_ids = jnp.arange(n_slc)
    in_blk = jnp.arange(SLC_BLOCK)

    def block(i):
        q0 = i * Q_BLOCK
        t = q0 + jnp.arange(Q_BLOCK)
        qb = lax.dynamic_slice_in_dim(q, q0, Q_BLOCK, axis=1)
        gb = lax.dynamic_slice_in_dim(gates, q0, Q_BLOCK, axis=1)
        s_c = jnp.einsum('bqhd,bnd->bhqn', qb, k_cmp) * scale
        valid_c = (cmp_end[None, :] <= t[:, None])[None, None]
        p_c = masked_softmax(s_c, valid_c)
        o_c = jnp.einsum('bhqn,bnd->bqhd', p_c.astype(dtype), v_cmp)
        imp = jnp.einsum('bhqn,nj->bqj', p_c, overlap)
        blk_t = t // SLC_BLOCK
        forced = (blk_ids[None, :] == 0) | (blk_ids[None, :] == blk_t[:, None])
        admissible = blk_ids[None, :] <= blk_t[:, None]
        imp = jnp.where(forced, jnp.inf, jnp.where(admissible, imp, -jnp.inf))
        _, idx = lax.top_k(imp, topn)
        k_sel = take_rows(ks_blocks, idx).reshape(B, Q_BLOCK, topn * SLC_BLOCK, HEAD_DIM)
        v_sel = take_rows(vs_blocks, idx).reshape(B, Q_BLOCK, topn * SLC_BLOCK, HEAD_DIM)
        pos_sel = (idx[..., None] * SLC_BLOCK + in_blk).reshape(B, Q_BLOCK, topn * SLC_BLOCK)
        s_s = jnp.einsum('bqhd,bqkd->bhqk', qb, k_sel) * scale
        p_s = masked_softmax(s_s, (pos_sel <= t[None, :, None])[:, None]).astype(dtype)
        o_s = jnp.einsum('bhqk,bqkd->bqhd', p_s, v_sel)
        kwb = lax.dynamic_slice_in_dim(kw_pad, q0, Q_BLOCK + WINDOW, axis=1)
        vwb = lax.dynamic_slice_in_dim(vw_pad, q0, Q_BLOCK + WINDOW, axis=1)
        pos_w = q0 - WINDOW + jnp.arange(Q_BLOCK + WINDOW)
        valid_w = ((pos_w[None, :] >= 0) & (pos_w[None, :] <= t[:, None])
                   & (t[:, None] - pos_w[None, :] < WINDOW))
        s_w = jnp.einsum('bqhd,bkd->bhqk', qb, kwb) * scale
        p_w = masked_softmax(s_w, valid_w[None, None]).astype(dtype)
        o_w = jnp.einsum('bhqk,bkd->bqhd', p_w, vwb)
        g = jax.nn.sigmoid(gb.astype(jnp.float32)).astype(dtype)
        return g[..., 0:1] * o_c + g[..., 1:2] * o_s + g[..., 2:3] * o_w

    out = lax.map(block, jnp.arange(T // Q_BLOCK))
    return jnp.moveaxis(out, 0, 1).reshape(B, T, -1)


def shortconv_mixer(b_gate, c_gate, xc, conv_w):
    T = xc.shape[1]
    u = c_gate * xc
    up = jnp.pad(u, ((0, 0), (CONV_WIDTH - 1, 0), (0, 0)))
    y = up[:, 0:T] * conv_w[0]
    for j in range(1, CONV_WIDTH):
        y = y + up[:, j:j + T] * conv_w[j]
    return b_gate * y


def pool_mixer(u, pool_w, pool_scale):
    B, T, _ = u.shape
    ug = u.reshape(B, T, POOL_GROUPS, POOL_CH)
    cs = jnp.cumsum(ug.astype(jnp.float32), axis=1)
    cs = jnp.pad(cs, ((0, 0), (1, 0), (0, 0), (0, 0)))
    win = jnp.array(POOL_WINDOWS, dtype=jnp.int32)
    t = jnp.arange(T)
    start = jnp.maximum(t[:, None] + 1 - win[None, :], 0)
    lower = cs[:, start, jnp.arange(POOL_GROUPS)[None, :]]
    count = jnp.minimum(t[:, None] + 1, win[None, :]).astype(jnp.float32)
    mean = (cs[:, 1:] - lower) / count[None, :, :, None]
    pooled = (mean - ug.astype(jnp.float32)).astype(u.dtype)
    y = jnp.einsum('btgc,gcd->btgd', pooled, pool_w).reshape(B, T, W_GROUP)
    return y * pool_scale


def hybrid_layer(x, norm_w, w_in, w_out, conv_w, pe_cmp, w_cmp_k, w_cmp_v, pool_w, pool_scale):
    B, T, _ = x.shape
    pos = jnp.arange(T)
    h = rmsnorm(x, norm_w)
    p = split_cols(h @ w_in)
    heads = lambda a, n: a.reshape(B, T, n, -1)
    one_head_rope = lambda a, rot: rope(a[:, :, None], pos, rot)[:, :, 0]
    o_a = dsa_mixer(rope(heads(p['a_q'], DSA_HEADS), pos, ROT_DIM), one_head_rope(p['a_k'], ROT_DIM),
                    p['a_v'], rope(heads(p['a_qi'], IDX_HEADS), pos, IDX_ROT),
                    one_head_rope(p['a_ki'], IDX_ROT), p['a_wi'])
    o_b = nsa_mixer(rope(heads(p['b_q'], NSA_HEADS), pos, ROT_DIM), p['b_kc'], p['b_vc'],
                    one_head_rope(p['b_ks'], ROT_DIM), p['b_vs'],
                    one_head_rope(p['b_kw'], ROT_DIM), p['b_vw'],
                    heads(p['b_g'], NSA_HEADS), pe_cmp, w_cmp_k, w_cmp_v)
    o_c = shortconv_mixer(p['c_b'], p['c_c'], p['c_x'], conv_w)
    o_d = pool_mixer(p['d_u'], pool_w, pool_scale)
    mixed = jnp.concatenate([jax.nn.silu(p['a_gate']) * o_a, jax.nn.silu(p['b_gate']) * o_b,
                             jax.nn.silu(p['c_gate']) * o_c, jax.nn.silu(p['d_gate']) * o_d], axis=-1)
    return x + mixed @ w_out


def setup_inputs(seed: int = 0) -> dict:
    key = jax.random.key(seed)
    ks = jax.random.split(key, 11)
    f32 = jnp.float32
    x = jax.random.normal(ks[0], (BATCH, SEQ, D_MODEL), f32)
    norm_w = 1.0 + 0.1 * jax.random.normal(ks[1], (DEPTH, D_MODEL), f32)
    w_in = jax.random.normal(ks[2], (DEPTH, D_MODEL, D_IN), f32) * D_MODEL ** -0.5
    w_out = jax.random.normal(ks[3], (DEPTH, D_MIX, D_MODEL), f32) * D_MIX ** -0.5
    conv_w = jax.random.normal(ks[4], (DEPTH, CONV_WIDTH, W_GROUP), f32) * CONV_WIDTH ** -0.5
    pe_cmp = 0.02 * jax.random.normal(ks[5], (DEPTH, CMP_LEN, HEAD_DIM), f32)
    w_cmp_k = jax.random.normal(ks[6], (DEPTH, CMP_LEN * HEAD_DIM, HEAD_DIM), f32) * (CMP_LEN * HEAD_DIM) ** -0.5
    w_cmp_v = jax.random.normal(ks[7], (DEPTH, CMP_LEN * HEAD_DIM, HEAD_DIM), f32) * (CMP_LEN * HEAD_DIM) ** -0.5
    pool_w = jax.random.normal(ks[8], (DEPTH, POOL_GROUPS, POOL_CH, POOL_CH), f32) * POOL_CH ** -0.5
    pool_scale = 1.0 + 0.1 * jax.random.normal(ks[9], (DEPTH, W_GROUP), f32)
    final_norm_w = 1.0 + 0.1 * jax.random.normal(ks[10], (D_MODEL,), f32)
    return {"x": x, "norm_w": norm_w, "w_in": w_in, "w_out": w_out, "conv_w": conv_w,
            "pe_cmp": pe_cmp, "w_cmp_k": w_cmp_k, "w_cmp_v": w_cmp_v, "pool_w": pool_w,
            "pool_scale": pool_scale, "final_norm_w": final_norm_w}


def reference(x, norm_w, w_in, w_out, conv_w, pe_cmp, w_cmp_k, w_cmp_v, pool_w, pool_scale, final_norm_w):
    for l in range(DEPTH):
        x = hybrid_layer(x, norm_w[l], w_in[l], w_out[l], conv_w[l], pe_cmp[l],
                         w_cmp_k[l], w_cmp_v[l], pool_w[l], pool_scale[l])
    return rmsnorm(x, final_norm_w)
```

```python
import functools

import numpy as np
import jax
import jax.numpy as jnp
from jax import lax
from jax.experimental import pallas as pl
from jax.experimental.pallas import tpu as pltpu

HEAD_DIM = 64
N_HEADS = 4
ROT_DIM = HEAD_DIM // 4
ROPE_THETA = 500000.0
EPS = 1e-6
IDX_HEADS = 8
IDX_DIM = 32
IDX_ROT = IDX_DIM // 4
DSA_TOPK = 256
CMP_LEN = 32
CMP_STRIDE = 16
SLC_BLOCK = 64
SLC_TOPN = 16
WINDOW = 512
CONV_WIDTH = 3
POOL_WINDOWS = (2, 4, 8, 16)
W_GROUP = 256

Q_BLOCK = 128
LANES = 128
HALO = 16
NEG = -1e30
VMEM_LIMIT = 48 * 1024 * 1024

F32 = jnp.float32
BF16 = jnp.bfloat16
I32 = jnp.int32
INT_MIN = -2 ** 31

N_K = 0
N_KVC = 256
N_GATE = 384
N_CB, N_CC, N_CX, N_CG, N_DU, N_DG = 896, 1152, 1408, 1664, 1920, 2176
N_NAT = 2432
T_QA, T_QI, T_QB, T_V, T_SMALL, N_TR = 0, 256, 512, 768, 960, 1024
KI_OFF = 192


def _silu(v):
    return v / (1.0 + jnp.exp(-v))


def _dot(a, b):
    return jnp.dot(a, b, preferred_element_type=F32)


def _proj_kernel(x_ref, nw_ref, wn_ref, wt_ref, tabn_ref, tabt_ref, convw_ref, poolw_ref,
                 pscale_ref, knat_o, kvc_o, gates_o, mcd_o, qaT_o, qiT_o, qbT_o, vT_o, smallT_o,
                 cu_ref, du_ref, *, tm):
    t_idx = pl.program_id(1)
    x = x_ref[0]
    ms = jnp.mean(x * x, axis=-1, keepdims=True)
    h = (x * lax.rsqrt(ms + EPS) * nw_ref[...]).astype(BF16)
    zn = _dot(h, wn_ref[...])
    zt = lax.dot_general(wt_ref[...], h, (((1,), (1,)), ((), ())),
                         preferred_element_type=F32)

    zk = zn[:, N_K:N_K + 256]
    kr = (zk * tabn_ref[0]
          + pltpu.roll(zk, 256 - 8, 1) * tabn_ref[1] + pltpu.roll(zk, 8, 1) * tabn_ref[2]
          + pltpu.roll(zk, 256 - 4, 1) * tabn_ref[3] + pltpu.roll(zk, 4, 1) * tabn_ref[4])
    knat_o[0] = kr.astype(BF16)
    kvc_o[0] = zn[:, N_KVC:N_KVC + 128]
    gates_o[0] = _silu(zn[:, N_GATE:N_GATE + 512])

    c_b = zn[:, N_CB:N_CB + 256]
    u = zn[:, N_CC:N_CC + 256] * zn[:, N_CX:N_CX + 256]
    d_u = zn[:, N_DU:N_DU + 256]

    @pl.when(t_idx == 0)
    def _():
        cu_ref[0:HALO, :] = jnp.zeros((HALO, 256), F32)
        du_ref[0:HALO, :] = jnp.zeros((HALO, 256), F32)

    @pl.when(t_idx > 0)
    def _():
        cu_ref[0:HALO, :] = cu_ref[tm:tm + HALO, :]
        du_ref[0:HALO, :] = du_ref[tm:tm + HALO, :]

    cu_ref[HALO:HALO + tm, :] = u
    du_ref[HALO:HALO + tm, :] = d_u
    cw = convw_ref[...]
    y = (cu_ref[HALO - 2:HALO - 2 + tm, :] * cw[0:1, :]
         + cu_ref[HALO - 1:HALO - 1 + tm, :] * cw[1:2, :] + u * cw[2:3, :])
    o_c = c_b * y

    acc = d_u
    sums = {}
    for k in range(1, max(POOL_WINDOWS)):
        acc = acc + du_ref[HALO - k:HALO - k + tm, :]
        if k + 1 in POOL_WINDOWS:
            sums[k + 1] = acc
    lane = lax.broadcasted_iota(I32, (tm, 256), 1)
    row = lax.broadcasted_iota(I32, (tm, 256), 0)
    pch = W_GROUP // len(POOL_WINDOWS)
    ssel = sums[POOL_WINDOWS[-1]]
    wl = jnp.full((tm, 256), POOL_WINDOWS[-1], I32)
    for g in range(len(POOL_WINDOWS) - 2, -1, -1):
        ssel = jnp.where(lane < (g + 1) * pch, sums[POOL_WINDOWS[g]], ssel)
        wl = jnp.where(lane < (g + 1) * pch, POOL_WINDOWS[g], wl)
    cnt = jnp.minimum(t_idx * tm + row + 1, wl).astype(F32)
    pooled = ssel / cnt - d_u
    o_d = _dot(pooled.astype(BF16), poolw_ref[...]) * pscale_ref[...]
    mcd_o[0] = jnp.concatenate(
        [_silu(zn[:, N_CG:N_CG + 256]) * o_c, _silu(zn[:, N_DG:N_DG + 256]) * o_d],
        axis=1).astype(BF16)

    c16, s16, c8, s8 = tabt_ref[0], tabt_ref[1], tabt_ref[2], tabt_ref[3]

    def rope_heads(base, out_ref):
        for hh in range(N_HEADS):
            b = base + HEAD_DIM * hh
            x1, x2 = zt[b:b + 8, :], zt[b + 8:b + 16, :]
            o = jnp.concatenate([x1 * c16 - x2 * s16, x2 * c16 + x1 * s16], axis=0)
            out_ref[0, HEAD_DIM * hh:HEAD_DIM * hh + 16, :] = o.astype(BF16)
            out_ref[0, HEAD_DIM * hh + 16:HEAD_DIM * (hh + 1), :] = zt[b + 16:b + HEAD_DIM, :].astype(BF16)

    rope_heads(T_QA, qaT_o)
    rope_heads(T_QB, qbT_o)
    for hh in range(IDX_HEADS):
        b = T_QI + IDX_DIM * hh
        x8 = zt[b:b + 8, :]
        o = x8 * c8 + pltpu.roll(x8, 4, 0) * s8
        qiT_o[0, IDX_DIM * hh:IDX_DIM * hh + 16, :] = jnp.concatenate(
            [o, zt[b + 8:b + 16, :]], axis=0).astype(BF16)
        qiT_o[0, IDX_DIM * hh + 16:IDX_DIM * (hh + 1), :] = zt[b + 16:b + IDX_DIM, :].astype(BF16)
    vT_o[0] = zt[T_V:T_V + 192, :].astype(BF16)
    smallT_o[0] = zt[T_SMALL:T_SMALL + 64, :]


def _proj_call(x, nw, wn, wt, tabn, tabt, convw, poolw, pscale, *, tm):
    B, T, D = x.shape
    grid = (B, T // tm)
    const = lambda *shape: pl.BlockSpec(shape, lambda b, t: (0,) * len(shape))
    nat = lambda w: pl.BlockSpec((1, tm, w), lambda b, t: (b, t, 0))
    tr = lambda r: pl.BlockSpec((1, r, tm), lambda b, t: (b, 0, t))
    out_shape = (
        jax.ShapeDtypeStruct((B, T, 256), BF16), jax.ShapeDtypeStruct((B, T, 128), F32),
        jax.ShapeDtypeStruct((B, T, 512), F32), jax.ShapeDtypeStruct((B, T, 512), BF16),
        jax.ShapeDtypeStruct((B, 256, T), BF16), jax.ShapeDtypeStruct((B, 256, T), BF16),
        jax.ShapeDtypeStruct((B, 256, T), BF16), jax.ShapeDtypeStruct((B, 192, T), BF16),
        jax.ShapeDtypeStruct((B, 64, T), F32))
    return pl.pallas_call(
        functools.partial(_proj_kernel, tm=tm),
        grid=grid,
        in_specs=[nat(D), const(1, D), const(D, N_NAT), const(N_TR, D),
                  pl.BlockSpec((5, tm, 256), lambda b, t: (0, t, 0)),
                  pl.BlockSpec((4, 8, tm), lambda b, t: (0, 0, t)),
                  const(CONV_WIDTH, 256), const(256, 256), const(1, 256)],
        out_specs=(nat(256), nat(128), nat(512), nat(512), tr(256), tr(256), tr(256), tr(192), tr(64)),
        out_shape=out_shape,
        scratch_shapes=[pltpu.VMEM((tm + HALO, 256), F32), pltpu.VMEM((tm + HALO, 256), F32)],
        compiler_params=pltpu.CompilerParams(
            dimension_semantics=("parallel", "arbitrary"), vmem_limit_bytes=VMEM_LIMIT),
        name="proj",
    )(x, nw, wn, wt, tabn, tabt, convw, poolw, pscale)


def _cmp_kernel(kvc_ref, pe_ref, wc_ref, tab_ref, kcmp_o, vcmpT_o, *, ncp):
    c = kvc_ref[0]
    first = _dot((c + pe_ref[0]).astype(BF16), wc_ref[0])
    second = _dot((c + pe_ref[1]).astype(BF16), wc_ref[1])
    kv = first + pltpu.roll(second, ncp - 1, 0)
    kr = (kv * tab_ref[0] + pltpu.roll(kv, 128 - 8, 1) * tab_ref[1]
          + pltpu.roll(kv, 8, 1) * tab_ref[2])
    kcmp_o[0] = kr[:, 0:HEAD_DIM].astype(BF16)
    vcmpT_o[0] = kr.T[HEAD_DIM:2 * HEAD_DIM, :].astype(BF16)


def _cmp_call(kvc_chunks, pe2, wc, tabc):
    B, ncp, width = kvc_chunks.shape
    return pl.pallas_call(
        functools.partial(_cmp_kernel, ncp=ncp),
        grid=(B,),
        in_specs=[pl.BlockSpec((1, ncp, width), lambda b: (b, 0, 0)),
                  pl.BlockSpec((2, 1, width), lambda b: (0, 0, 0)),
                  pl.BlockSpec((2, width, 128), lambda b: (0, 0, 0)),
                  pl.BlockSpec((3, ncp, 128), lambda b: (0, 0, 0))],
        out_specs=(pl.BlockSpec((1, ncp, HEAD_DIM), lambda b: (b, 0, 0)),
                   pl.BlockSpec((1, HEAD_DIM, ncp), lambda b: (b, 0, 0))),
        out_shape=(jax.ShapeDtypeStruct((B, ncp, HEAD_DIM), BF16),
                   jax.ShapeDtypeStruct((B, HEAD_DIM, ncp), BF16)),
        compiler_params=pltpu.CompilerParams(
            dimension_semantics=("parallel",), vmem_limit_bytes=VMEM_LIMIT),
        name="compress",
    )(kvc_chunks, pe2, wc, tabc)


def _heads_on_lanes(qT_ref):
    return jnp.concatenate(
        [qT_ref[0, HEAD_DIM * hh:HEAD_DIM * (hh + 1), :] for hh in range(N_HEADS)], axis=1)


def _rep_heads(a):
    return jnp.concatenate([a] * N_HEADS, axis=1)


def _masked_attention(qT, k_load, vT_load, sel_fn, lo, hi, tk):
    width = N_HEADS * Q_BLOCK

    def body(kt, carry):
        m, l, acc = carry
        k0 = pl.multiple_of(kt * tk, tk)
        s = _dot(k_load(k0), qT)
        keep = _rep_heads(sel_fn(k0)) > 0
        sm = jnp.where(keep, s, NEG)
        m_new = jnp.maximum(m, jnp.max(sm, axis=0, keepdims=True))
        alpha = jnp.exp(m - m_new)
        p = jnp.where(keep, jnp.exp(sm - m_new), 0.0)
        l = alpha * l + jnp.sum(p, axis=0, keepdims=True)
        acc = alpha * acc + _dot(vT_load(k0), p.astype(BF16))
        return m_new, l, acc

    init = (jnp.full((1, width), NEG, F32), jnp.zeros((1, width), F32),
            jnp.zeros((HEAD_DIM, width), F32))
    _, l, acc = lax.fori_loop(lo, hi, body, init)
    return acc / jnp.maximum(l, 1e-30)


def _to_token_major(oT):
    stacked = jnp.concatenate(
        [oT[:, Q_BLOCK * hh:Q_BLOCK * (hh + 1)] for hh in range(N_HEADS)], axis=0)
    return stacked.T


def _dsa_kernel(qaT_ref, qiT_ref, smallT_ref, ga_ref, knat_ref, vT_ref, out_ref,
                keys_ref, j_ref, *, tk, topk, jbits):
    i = pl.program_id(1)
    q0 = i * Q_BLOCK
    n_kt = (q0 + Q_BLOCK + tk - 1) // tk
    t_row = q0 + lax.broadcasted_iota(I32, (1, Q_BLOCK), 1)
    row_iota = lax.broadcasted_iota(I32, (tk, Q_BLOCK), 0)

    qi_cat = jnp.concatenate(
        [qiT_ref[0, IDX_DIM * hh:IDX_DIM * (hh + 1), :] for hh in range(IDX_HEADS)], axis=1)
    wi = smallT_ref[0, 0:IDX_HEADS, :]

    def score_body(kt, carry):
        k0 = pl.multiple_of(kt * tk, tk)
        ki = knat_ref[0, pl.ds(k0, tk), KI_OFF:KI_OFF + IDX_DIM]
        lg = _dot(ki, qi_cat)
        sc = jnp.zeros((tk, Q_BLOCK), F32)
        for hh in range(IDX_HEADS):
            sc = sc + jnp.maximum(lg[:, Q_BLOCK * hh:Q_BLOCK * (hh + 1)], 0.0) * wi[hh:hh + 1, :]
        sc = jnp.where(sc == 0.0, 0.0, sc)
        bits = lax.bitcast_convert_type(sc, I32)
        key = bits ^ ((bits >> 31) & 0x7FFFFFFF)
        keys_ref[pl.ds(k0, tk), :] = jnp.where(k0 + row_iota <= t_row, key, INT_MIN)
        return carry

    lax.fori_loop(0, n_kt, score_body, 0)

    def count(pred):
        def body(kt, acc):
            k0 = pl.multiple_of(kt * tk, tk)
            hit = pred(keys_ref[pl.ds(k0, tk), :], k0)
            return acc + jnp.sum(hit.reshape(tk // 8, 8, Q_BLOCK), axis=0)
        acc = lax.fori_loop(0, n_kt, body, jnp.zeros((8, Q_BLOCK), I32))
        return jnp.sum(acc, axis=0, keepdims=True)

    def thr_body(it, prefix):
        cand = prefix ^ lax.shift_left(jnp.int32(1), 31 - it)
        cnt = count(lambda blk, k0: jnp.where(blk >= cand, 1, 0))
        return jnp.where(cnt >= topk, cand, prefix)

    thr = lax.fori_loop(0, 32, thr_body, jnp.full((1, Q_BLOCK), INT_MIN, I32))

    cnt_ge = count(lambda blk, k0: jnp.where(blk >= thr, 1, 0))
    j_ref[...] = jnp.full((1, Q_BLOCK), 2 ** 30, I32)

    @pl.when(jnp.max(cnt_ge) > topk)
    def _():
        need = topk - count(lambda blk, k0: jnp.where(blk > thr, 1, 0))

        def j_body(it, jp):
            cand = jp | lax.shift_left(jnp.int32(1), jbits - 1 - it)
            before = count(lambda blk, k0: jnp.where(
                blk == thr, jnp.where(k0 + row_iota < cand, 1, 0), 0))
            return jnp.where(before < need, cand, jp)

        j_ref[...] = lax.fori_loop(0, jbits, j_body, jnp.zeros((1, Q_BLOCK), I32))

    j_eff = jnp.minimum(j_ref[...], t_row)

    def sel_fn(k0):
        key = keys_ref[pl.ds(k0, tk), :]
        return jnp.where(key > thr, 1,
                         jnp.where(key == thr, jnp.where(k0 + row_iota <= j_eff, 1, 0), 0))

    oT = _masked_attention(
        _heads_on_lanes(qaT_ref),
        lambda k0: knat_ref[0, pl.ds(k0, tk), 0:HEAD_DIM],
        lambda k0: vT_ref[0, 0:HEAD_DIM, pl.ds(k0, tk)],
        sel_fn, 0, n_kt, tk)
    out_ref[0] = (_to_token_major(oT) * ga_ref[0]).astype(BF16)


def _dsa_call(qaT, qiT, smallT, gates, knat, vT, *, tk):
    B, _, T = qaT.shape
    qblk = lambda r: pl.BlockSpec((1, r, Q_BLOCK), lambda b, i: (b, 0, i))
    return pl.pallas_call(
        functools.partial(_dsa_kernel, tk=tk, topk=min(DSA_TOPK, T // 4), jbits=T.bit_length()),
        grid=(B, T // Q_BLOCK),
        in_specs=[qblk(256), qblk(256), qblk(64),
                  pl.BlockSpec((1, Q_BLOCK, 256), lambda b, i: (b, i, 0)),
                  pl.BlockSpec((1, T, 256), lambda b, i: (b, 0, 0)),
                  pl.BlockSpec((1, 192, T), lambda b, i: (b, 0, 0))],
        out_specs=pl.BlockSpec((1, Q_BLOCK, 256), lambda b, i: (b, i, 0)),
        out_shape=jax.ShapeDtypeStruct((B, T, 256), BF16),
        scratch_shapes=[pltpu.VMEM((T, Q_BLOCK), I32), pltpu.VMEM((1, Q_BLOCK), I32)],
        compiler_params=pltpu.CompilerParams(
            dimension_semantics=("parallel", "arbitrary"), vmem_limit_bytes=VMEM_LIMIT),
        name="dsa",
    )(qaT, qiT, smallT, gates, knat, vT)


def _nsa_kernel(qbT_ref, smallT_ref, gb_ref, knat_ref, vT_ref, kcmp_ref, vcmpT_ref, ovT_ref,
                out_ref, selx_ref, *, tk, seq):
    i = pl.program_id(1)
    q0 = i * Q_BLOCK
    ncp = seq // CMP_STRIDE
    nslc = seq // SLC_BLOCK
    topn = min(SLC_TOPN, nslc)
    t_row = q0 + lax.broadcasted_iota(I32, (1, Q_BLOCK), 1)
    qT = _heads_on_lanes(qbT_ref)

    s_c = _dot(kcmp_ref[0], qT)
    n_iota = lax.broadcasted_iota(I32, (ncp, Q_BLOCK), 0)
    cend = n_iota * CMP_STRIDE + (CMP_LEN - 1)
    keep = _rep_heads(jnp.where(cend <= t_row, 1, 0)) > 0
    sm = jnp.where(keep, s_c, NEG)
    p_c = jnp.where(keep, jnp.exp(sm - jnp.max(sm, axis=0, keepdims=True)), 0.0)
    p_c = p_c / jnp.maximum(jnp.sum(p_c, axis=0, keepdims=True), 1e-30)
    o_cmp = _dot(vcmpT_ref[0], p_c.astype(BF16))

    psum = p_c[:, 0:Q_BLOCK]
    for hh in range(1, N_HEADS):
        psum = psum + p_c[:, Q_BLOCK * hh:Q_BLOCK * (hh + 1)]
    imp = jnp.dot(ovT_ref[...], psum, preferred_element_type=F32,
                  precision=lax.Precision.HIGHEST)
    jidx = lax.broadcasted_iota(I32, (nslc, Q_BLOCK), 0)
    blk_t = t_row // SLC_BLOCK
    v = jnp.where(jidx == 0, jnp.inf,
                  jnp.where(jidx == blk_t, jnp.inf, jnp.where(jidx <= blk_t, imp, -jnp.inf)))
    rank = jnp.zeros((nslc, Q_BLOCK), I32)
    for jp in range(nslc):
        r = v[jp:jp + 1, :]
        rank = rank + jnp.where(r > v, 1, jnp.where(r == v, jnp.where(jidx > jp, 1, 0), 0))
    sel = jnp.where(rank < topn, jnp.where(jidx <= blk_t, 1, 0), 0)
    for j in range(nslc):
        selx_ref[SLC_BLOCK * j:SLC_BLOCK * (j + 1), :] = jnp.broadcast_to(
            sel[j:j + 1, :], (SLC_BLOCK, Q_BLOCK))

    row_s = lax.broadcasted_iota(I32, (tk, Q_BLOCK), 0)
    o_slc = _masked_attention(
        qT,
        lambda k0: knat_ref[0, pl.ds(k0, tk), HEAD_DIM:2 * HEAD_DIM],
        lambda k0: vT_ref[0, HEAD_DIM:2 * HEAD_DIM, pl.ds(k0, tk)],
        lambda k0: jnp.where(k0 + row_s <= t_row, selx_ref[pl.ds(k0, tk), :], 0),
        0, (q0 + Q_BLOCK + tk - 1) // tk, tk)

    tw = Q_BLOCK
    row_w = lax.broadcasted_iota(I32, (tw, Q_BLOCK), 0)

    def win_sel(k0):
        pos = k0 + row_w
        return jnp.where(pos <= t_row, jnp.where(t_row - pos < WINDOW, 1, 0), 0)

    o_win = _masked_attention(
        qT,
        lambda k0: knat_ref[0, pl.ds(k0, tw), 2 * HEAD_DIM:3 * HEAD_DIM],
        lambda k0: vT_ref[0, 2 * HEAD_DIM:3 * HEAD_DIM, pl.ds(k0, tw)],
        win_sel, jnp.maximum(i - WINDOW // tw, 0), i + 1, tw)

    g = 1.0 / (1.0 + jnp.exp(-smallT_ref[0, IDX_HEADS:IDX_HEADS + 3 * N_HEADS, :]))
    parts = []
    for hh in range(N_HEADS):
        cols = slice(Q_BLOCK * hh, Q_BLOCK * (hh + 1))
        parts.append(g[3 * hh:3 * hh + 1, :] * o_cmp[:, cols]
                     + g[3 * hh + 1:3 * hh + 2, :] * o_slc[:, cols]
                     + g[3 * hh + 2:3 * hh + 3, :] * o_win[:, cols])
    o_tok = jnp.concatenate(parts, axis=0).T
    out_ref[0] = (o_tok * gb_ref[0]).astype(BF16)


def _nsa_call(qbT, smallT, gates, knat, vT, kcmp, vcmpT, ovT, *, tk):
    B, _, T = qbT.shape
    ncp, nslc = T // CMP_STRIDE, T // SLC_BLOCK
    qblk = lambda r: pl.BlockSpec((1, r, Q_BLOCK), lambda b, i: (b, 0, i))
    return pl.pallas_call(
        functools.partial(_nsa_kernel, tk=tk, seq=T),
        grid=(B, T // Q_BLOCK),
        in_specs=[qblk(256), qblk(64),
                  pl.BlockSpec((1, Q_BLOCK, 256), lambda b, i: (b, i, 1)),
                  pl.BlockSpec((1, T, 256), lambda b, i: (b, 0, 0)),
                  pl.BlockSpec((1, 192, T), lambda b, i: (b, 0, 0)),
                  pl.BlockSpec((1, ncp, HEAD_DIM), lambda b, i: (b, 0, 0)),
                  pl.BlockSpec((1, HEAD_DIM, ncp), lambda b, i: (b, 0, 0)),
                  pl.BlockSpec((nslc, ncp), lambda b, i: (0, 0))],
        out_specs=pl.BlockSpec((1, Q_BLOCK, 256), lambda b, i: (b, i, 0)),
        out_shape=jax.ShapeDtypeStruct((B, T, 256), BF16),
        scratch_shapes=[pltpu.VMEM((T, Q_BLOCK), I32)],
        compiler_params=pltpu.CompilerParams(
            dimension_semantics=("parallel", "arbitrary"), vmem_limit_bytes=VMEM_LIMIT),
        name="nsa",
    )(qbT, smallT, gates, knat, vT, kcmp, vcmpT, ovT)


def _out_kernel(x_ref, ma_ref, mb_ref, mcd_ref, wo_ref, fw_ref, o_ref, *, final):
    y = (x_ref[...] + _dot(ma_ref[...], wo_ref[0:256, :]) + _dot(mb_ref[...], wo_ref[256:512, :])
         + _dot(mcd_ref[...], wo_ref[512:1024, :]))
    if final:
        ms = jnp.mean(y * y, axis=-1, keepdims=True)
        y = y * lax.rsqrt(ms + EPS) * fw_ref[...]
    o_ref[...] = y


def _out_call(x2, ma, mb, mcd, wo, fw, *, final, tm):
    R, D = x2.shape
    rows = lambda w: pl.BlockSpec((tm, w), lambda r: (r, 0))
    return pl.pallas_call(
        functools.partial(_out_kernel, final=final),
        grid=(R // tm,),
        in_specs=[rows(D), rows(256), rows(256), rows(512),
                  pl.BlockSpec((D, D), lambda r: (0, 0)), pl.BlockSpec((1, D), lambda r: (0, 0))],
        out_specs=rows(D),
        out_shape=jax.ShapeDtypeStruct((R, D), F32),
        compiler_params=pltpu.CompilerParams(
            dimension_semantics=("parallel",), vmem_limit_bytes=VMEM_LIMIT),
        name="outproj",
    )(x2, ma, mb, mcd, wo, fw)


def _rope_angles(pos, rot):
    half = rot // 2
    inv = ROPE_THETA ** (-jnp.arange(half, dtype=F32) / half)
    ang = pos.astype(F32)[:, None] * inv[None, :]
    return jnp.cos(ang), jnp.sin(ang)


def _rope_tables(T):
    pos = jnp.arange(T)
    c16, s16 = _rope_angles(pos, ROT_DIM)
    c8, s8 = _rope_angles(pos, IDX_ROT)
    ones, zeros = jnp.ones((T, 256), F32), jnp.zeros((T, 256), F32)
    C, Sa, Sb, Sc, Sd = ones, zeros, zeros, zeros, zeros
    for off in (0, 64, 128):
        C = C.at[:, off:off + 8].set(c16).at[:, off + 8:off + 16].set(c16)
        Sa = Sa.at[:, off:off + 8].set(-s16)
        Sb = Sb.at[:, off + 8:off + 16].set(s16)
    C = C.at[:, KI_OFF:KI_OFF + 4].set(c8).at[:, KI_OFF + 4:KI_OFF + 8].set(c8)
    Sc = Sc.at[:, KI_OFF:KI_OFF + 4].set(-s8)
    Sd = Sd.at[:, KI_OFF + 4:KI_OFF + 8].set(s8)
    tabn = jnp.stack([C, Sa, Sb, Sc, Sd])
    tabt = jnp.stack([c16.T, s16.T, jnp.concatenate([c8.T, c8.T]),
                      jnp.concatenate([-s8.T, s8.T])])
    ncp = T // CMP_STRIDE
    cc, sc_ = _rope_angles(jnp.arange(ncp) * CMP_STRIDE + CMP_LEN - 1, ROT_DIM)
    Cc = jnp.ones((ncp, 128), F32).at[:, 0:8].set(cc).at[:, 8:16].set(cc)
    Sac = jnp.zeros((ncp, 128), F32).at[:, 0:8].set(-sc_)
    Sbc = jnp.zeros((ncp, 128), F32).at[:, 8:16].set(sc_)
    return tabn, tabt, jnp.stack([Cc, Sac, Sbc])


def _split_offsets():
    widths = (("a_q", 256), ("a_k", 64), ("a_v", 64), ("a_qi", 256), ("a_ki", 32), ("a_wi", 8),
              ("a_gate", 256), ("b_q", 256), ("b_kc", 64), ("b_vc", 64), ("b_ks", 64),
              ("b_vs", 64), ("b_kw", 64), ("b_vw", 64), ("b_g", 12), ("b_gate", 256),
              ("c_b", 256), ("c_c", 256), ("c_x", 256), ("c_gate", 256), ("d_u", 256),
              ("d_gate", 256))
    offs, o = {}, 0
    for name, w in widths:
        offs[name] = (o, o + w)
        o += w
    return offs


def _layer_weights(w_in, pool_w, pe_cmp, w_cmp_k, w_cmp_v):
    offs = _split_offsets()
    col = lambda n: w_in[:, offs[n][0]:offs[n][1]]
    D = w_in.shape[0]
    wn = jnp.concatenate(
        [col("a_k"), col("b_ks"), col("b_kw"), col("a_ki"), jnp.zeros((D, 32), F32),
         col("b_kc"), col("b_vc"), col("a_gate"), col("b_gate"), col("c_b"), col("c_c"),
         col("c_x"), col("c_gate"), col("d_u"), col("d_gate")], axis=1).astype(BF16)
    qscale = HEAD_DIM ** -0.5
    iscale = (IDX_DIM ** -0.5) * (IDX_HEADS ** -0.5)
    wt = jnp.concatenate(
        [col("a_q") * qscale, col("a_qi"), col("b_q") * qscale, col("a_v"), col("b_vs"),
         col("b_vw"), col("a_wi") * iscale, col("b_g"), jnp.zeros((D, 44), F32)],
        axis=1).T.astype(BF16)
    groups = pool_w.shape[0]
    pch = pool_w.shape[1]
    poolw = jnp.zeros((groups * pch, groups * pch), F32)
    for g in range(groups):
        poolw = poolw.at[g * pch:(g + 1) * pch, g * pch:(g + 1) * pch].set(pool_w[g])
    half = CMP_LEN // 2
    wk3 = w_cmp_k.reshape(CMP_LEN, HEAD_DIM, HEAD_DIM)
    wv3 = w_cmp_v.reshape(CMP_LEN, HEAD_DIM, HEAD_DIM)

    def chunk_w(lo):
        blk = jnp.zeros((half, 128, 128), F32)
        blk = blk.at[:, 0:64, 0:64].set(wk3[lo:lo + half]).at[:, 64:128, 64:128].set(wv3[lo:lo + half])
        return blk.reshape(half * 128, 128)

    wc = jnp.stack([chunk_w(0), chunk_w(half)]).astype(BF16)
    pe_rows = lambda lo: jnp.concatenate([pe_cmp[lo:lo + half]] * 2, axis=1).reshape(1, half * 128)
    pe2 = jnp.stack([pe_rows(0), pe_rows(half)])
    return wn, wt, poolw.astype(BF16), wc, pe2


def _overlap_T(T):
    ncp, nslc = T // CMP_STRIDE, T // SLC_BLOCK
    n_cmp = (T - CMP_LEN) // CMP_STRIDE + 1
    cs = np.arange(ncp) * CMP_STRIDE
    ce = cs + CMP_LEN - 1
    ss = np.arange(nslc) * SLC_BLOCK
    ov = (cs[None, :] < ss[:, None] + SLC_BLOCK) & (ce[None, :] >= ss[:, None]) & (np.arange(ncp)[None, :] < n_cmp)
    return jnp.asarray(ov.astype(np.float32))


def kernel(x, norm_w, w_in, w_out, conv_w, pe_cmp, w_cmp_k, w_cmp_v, pool_w, pool_scale, final_norm_w):
    B, T, D = x.shape
    depth = w_in.shape[0]
    tm = min(512, T)
    tk = min(256, T)
    tabn, tabt, tabc = _rope_tables(T)
    ovT = _overlap_T(T)
    fw = final_norm_w.reshape(1, D)
    for l in range(depth):
        wn, wt, poolw, wc, pe2 = _layer_weights(w_in[l], pool_w[l], pe_cmp[l], w_cmp_k[l], w_cmp_v[l])
        knat, kvc, gates, mcd, qaT, qiT, qbT, vT, smallT = _proj_call(
            x, norm_w[l].reshape(1, D), wn, wt, tabn, tabt, conv_w[l], poolw,
            pool_scale[l].reshape(1, W_GROUP), tm=tm)
        kcmp, vcmpT = _cmp_call(kvc.reshape(B, T // CMP_STRIDE, CMP_STRIDE * 128), pe2, wc, tabc)
        ma = _dsa_call(qaT, qiT, smallT, gates, knat, vT, tk=tk)
        mb = _nsa_call(qbT, smallT, gates, knat, vT, kcmp, vcmpT, ovT, tk=tk)
        x = _out_call(x.reshape(B * T, D), ma.reshape(B * T, 256), mb.reshape(B * T, 256),
                      mcd.reshape(B * T, 512), w_out[l].astype(BF16), fw,
                      final=(l == depth - 1), tm=tm).reshape(B, T, D)
    return x
```

```python
import functools

import numpy as np
import jax
import jax.numpy as jnp
from jax import lax
from jax.experimental import pallas as pl
from jax.experimental.pallas import tpu as pltpu

HEAD_DIM = 64
N_HEADS = 4
ROT_DIM = HEAD_DIM // 4
ROPE_THETA = 500000.0
EPS = 1e-6
IDX_HEADS = 8
IDX_DIM = 32
IDX_ROT = IDX_DIM // 4
DSA_TOPK = 256
CMP_LEN = 32
CMP_STRIDE = 16
SLC_BLOCK = 64
SLC_TOPN = 16
WINDOW = 512
CONV_WIDTH = 3
POOL_WINDOWS = (2, 4, 8, 16)
W_GROUP = 256

Q_BLOCK = 128
LANES = 128
HALO = 16
NEG = -1e30
M_INIT = -1e29
LOG2E = 1.4426950408889634
VMEM_LIMIT = 48 * 1024 * 1024

F32 = jnp.float32
BF16 = jnp.bfloat16
I32 = jnp.int32
INT_MIN = -2 ** 31

N_K = 0
N_KVC = 256
N_GATE = 384
N_CB, N_CC, N_CX, N_CG, N_DU, N_DG = 896, 1152, 1408, 1664, 1920, 2176
N_NAT = 2432
T_QA, T_QI, T_QB, T_V, T_SMALL, N_TR = 0, 256, 512, 768, 960, 1024
KI_OFF = 192


def _silu(v):
    return v / (1.0 + jnp.exp(-v))


def _dot(a, b):
    return jnp.dot(a, b, preferred_element_type=F32)


def _proj_kernel(x_ref, nw_ref, wn_ref, wt_ref, tabn_ref, tabt_ref, convw_ref, poolw_ref,
                 pscale_ref, knat_o, kvc_o, gates_o, mcd_o, qaT_o, qiT_o, qbT_o, vT_o, smallT_o,
                 cu_ref, du_ref, *, tm):
    t_idx = pl.program_id(1)
    x = x_ref[0]
    ms = jnp.mean(x * x, axis=-1, keepdims=True)
    h = (x * lax.rsqrt(ms + EPS) * nw_ref[...]).astype(BF16)
    zn = _dot(h, wn_ref[...])
    zt = lax.dot_general(wt_ref[...], h, (((1,), (1,)), ((), ())),
                         preferred_element_type=F32)

    zk = zn[:, N_K:N_K + 256]
    kr = (zk * tabn_ref[0]
          + pltpu.roll(zk, 256 - 8, 1) * tabn_ref[1] + pltpu.roll(zk, 8, 1) * tabn_ref[2]
          + pltpu.roll(zk, 256 - 4, 1) * tabn_ref[3] + pltpu.roll(zk, 4, 1) * tabn_ref[4])
    knat_o[0] = kr.astype(BF16)
    kvc_o[0] = zn[:, N_KVC:N_KVC + 128]
    gates_o[0] = _silu(zn[:, N_GATE:N_GATE + 512])

    c_b = zn[:, N_CB:N_CB + 256]
    u = zn[:, N_CC:N_CC + 256] * zn[:, N_CX:N_CX + 256]
    d_u = zn[:, N_DU:N_DU + 256]

    @pl.when(t_idx == 0)
    def _():
        cu_ref[0:HALO, :] = jnp.zeros((HALO, 256), F32)
        du_ref[0:HALO, :] = jnp.zeros((HALO, 256), F32)

    @pl.when(t_idx > 0)
    def _():
        cu_ref[0:HALO, :] = cu_ref[tm:tm + HALO, :]
        du_ref[0:HALO, :] = du_ref[tm:tm + HALO, :]

    cu_ref[HALO:HALO + tm, :] = u
    du_ref[HALO:HALO + tm, :] = d_u
    cw = convw_ref[...]
    y = (cu_ref[HALO - 2:HALO - 2 + tm, :] * cw[0:1, :]
         + cu_ref[HALO - 1:HALO - 1 + tm, :] * cw[1:2, :] + u * cw[2:3, :])
    o_c = c_b * y

    acc = d_u
    sums = {}
    for k in range(1, max(POOL_WINDOWS)):
        acc = acc + du_ref[HALO - k:HALO - k + tm, :]
        if k + 1 in POOL_WINDOWS:
            sums[k + 1] = acc
    lane = lax.broadcasted_iota(I32, (tm, 256), 1)
    row = lax.broadcasted_iota(I32, (tm, 256), 0)
    pch = W_GROUP // len(POOL_WINDOWS)
    ssel = sums[POOL_WINDOWS[-1]]
    wl = jnp.full((tm, 256), POOL_WINDOWS[-1], I32)
    for g in range(len(POOL_WINDOWS) - 2, -1, -1):
        ssel = jnp.where(lane < (g + 1) * pch, sums[POOL_WINDOWS[g]], ssel)
        wl = jnp.where(lane < (g + 1) * pch, POOL_WINDOWS[g], wl)
    cnt = jnp.minimum(t_idx * tm + row + 1, wl).astype(F32)
    pooled = ssel / cnt - d_u
    o_d = _dot(pooled.astype(BF16), poolw_ref[...]) * pscale_ref[...]
    mcd_o[0] = jnp.concatenate(
        [_silu(zn[:, N_CG:N_CG + 256]) * o_c, _silu(zn[:, N_DG:N_DG + 256]) * o_d],
        axis=1).astype(BF16)

    c16, s16, c8, s8 = tabt_ref[0], tabt_ref[1], tabt_ref[2], tabt_ref[3]

    def rope_heads(base, out_ref):
        for hh in range(N_HEADS):
            b = base + HEAD_DIM * hh
            x1, x2 = zt[b:b + 8, :], zt[b + 8:b + 16, :]
            o = jnp.concatenate([x1 * c16 - x2 * s16, x2 * c16 + x1 * s16], axis=0)
            out_ref[0, HEAD_DIM * hh:HEAD_DIM * hh + 16, :] = o.astype(BF16)
            out_ref[0, HEAD_DIM * hh + 16:HEAD_DIM * (hh + 1), :] = zt[b + 16:b + HEAD_DIM, :].astype(BF16)

    rope_heads(T_QA, qaT_o)
    rope_heads(T_QB, qbT_o)
    for hh in range(IDX_HEADS):
        b = T_QI + IDX_DIM * hh
        x8 = zt[b:b + 8, :]
        o = x8 * c8 + pltpu.roll(x8, 4, 0) * s8
        qiT_o[0, IDX_DIM * hh:IDX_DIM * hh + 16, :] = jnp.concatenate(
            [o, zt[b + 8:b + 16, :]], axis=0).astype(BF16)
        qiT_o[0, IDX_DIM * hh + 16:IDX_DIM * (hh + 1), :] = zt[b + 16:b + IDX_DIM, :].astype(BF16)
    vT_o[0] = zt[T_V:T_V + 192, :].astype(BF16)
    smallT_o[0] = zt[T_SMALL:T_SMALL + 64, :]


def _proj_call(x, nw, wn, wt, tabn, tabt, convw, poolw, pscale, *, tm):
    B, T, D = x.shape
    grid = (B, T // tm)
    const = lambda *shape: pl.BlockSpec(shape, lambda b, t: (0,) * len(shape))
    nat = lambda w: pl.BlockSpec((1, tm, w), lambda b, t: (b, t, 0))
    tr = lambda r: pl.BlockSpec((1, r, tm), lambda b, t: (b, 0, t))
    out_shape = (
        jax.ShapeDtypeStruct((B, T, 256), BF16), jax.ShapeDtypeStruct((B, T, 128), F32),
        jax.ShapeDtypeStruct((B, T, 512), F32), jax.ShapeDtypeStruct((B, T, 512), BF16),
        jax.ShapeDtypeStruct((B, 256, T), BF16), jax.ShapeDtypeStruct((B, 256, T), BF16),
        jax.ShapeDtypeStruct((B, 256, T), BF16), jax.ShapeDtypeStruct((B, 192, T), BF16),
        jax.ShapeDtypeStruct((B, 64, T), F32))
    return pl.pallas_call(
        functools.partial(_proj_kernel, tm=tm),
        grid=grid,
        in_specs=[nat(D), const(1, D), const(D, N_NAT), const(N_TR, D),
                  pl.BlockSpec((5, tm, 256), lambda b, t: (0, t, 0)),
                  pl.BlockSpec((4, 8, tm), lambda b, t: (0, 0, t)),
                  const(CONV_WIDTH, 256), const(256, 256), const(1, 256)],
        out_specs=(nat(256), nat(128), nat(512), nat(512), tr(256), tr(256), tr(256), tr(192), tr(64)),
        out_shape=out_shape,
        scratch_shapes=[pltpu.VMEM((tm + HALO, 256), F32), pltpu.VMEM((tm + HALO, 256), F32)],
        compiler_params=pltpu.CompilerParams(
            dimension_semantics=("parallel", "arbitrary"), vmem_limit_bytes=VMEM_LIMIT),
        name="proj",
    )(x, nw, wn, wt, tabn, tabt, convw, poolw, pscale)


def _cmp_kernel(kvc_ref, pe_ref, wc_ref, tab_ref, kcmp_o, vcmpT_o, *, ncp):
    c = kvc_ref[0]
    first = _dot((c + pe_ref[0]).astype(BF16), wc_ref[0])
    second = _dot((c + pe_ref[1]).astype(BF16), wc_ref[1])
    kv = first + pltpu.roll(second, ncp - 1, 0)
    kr = (kv * tab_ref[0] + pltpu.roll(kv, 128 - 8, 1) * tab_ref[1]
          + pltpu.roll(kv, 8, 1) * tab_ref[2])
    kcmp_o[0] = kr[:, 0:HEAD_DIM].astype(BF16)
    vcmpT_o[0] = kr.T[HEAD_DIM:2 * HEAD_DIM, :].astype(BF16)


def _cmp_call(kvc_chunks, pe2, wc, tabc):
    B, ncp, width = kvc_chunks.shape
    return pl.pallas_call(
        functools.partial(_cmp_kernel, ncp=ncp),
        grid=(B,),
        in_specs=[pl.BlockSpec((1, ncp, width), lambda b: (b, 0, 0)),
                  pl.BlockSpec((2, 1, width), lambda b: (0, 0, 0)),
                  pl.BlockSpec((2, width, 128), lambda b: (0, 0, 0)),
                  pl.BlockSpec((3, ncp, 128), lambda b: (0, 0, 0))],
        out_specs=(pl.BlockSpec((1, ncp, HEAD_DIM), lambda b: (b, 0, 0)),
                   pl.BlockSpec((1, HEAD_DIM, ncp), lambda b: (b, 0, 0))),
        out_shape=(jax.ShapeDtypeStruct((B, ncp, HEAD_DIM), BF16),
                   jax.ShapeDtypeStruct((B, HEAD_DIM, ncp), BF16)),
        compiler_params=pltpu.CompilerParams(
            dimension_semantics=("parallel",), vmem_limit_bytes=VMEM_LIMIT),
        name="compress",
    )(kvc_chunks, pe2, wc, tabc)


def _heads_on_lanes(qT_ref):
    return jnp.concatenate(
        [qT_ref[0, HEAD_DIM * hh:HEAD_DIM * (hh + 1), :] for hh in range(N_HEADS)], axis=1)


def _rep_heads(a):
    return jnp.concatenate([a] * N_HEADS, axis=1)


def _masked_attention(qT, k_load, vT_load, bias_fn, lo, hi, tk, s_ref, p_ref, acc_ref):
    width = N_HEADS * Q_BLOCK
    last = hi - 1

    def tile_start(kt):
        return pl.multiple_of(jnp.minimum(kt, last) * tk, tk)

    def scores(kt):
        return _dot(k_load(tile_start(kt)), qT)

    def values(kt, slot, alpha):
        acc_ref[...] = alpha * acc_ref[...] + _dot(vT_load(tile_start(kt)), p_ref[slot])

    def step(kt, slot, carry):
        m, l, alpha_prev = carry
        values(jnp.maximum(kt - 1, lo), 1 - slot, alpha_prev)
        s = s_ref[slot]
        s_ref[1 - slot] = scores(kt + 1)
        bias = jnp.where(kt <= last, bias_fn(tile_start(kt)), NEG)
        sm = s + _rep_heads(bias)
        m_new = jnp.maximum(m, jnp.max(sm, axis=0, keepdims=True))
        alpha = jnp.exp2(m - m_new)
        p = jnp.exp2(sm - m_new)
        p_ref[slot] = p.astype(BF16)
        return m_new, alpha * l + jnp.sum(p, axis=0, keepdims=True), alpha

    s_ref[0] = scores(lo)
    p_ref[1] = jnp.zeros((tk, width), BF16)
    acc_ref[...] = jnp.zeros((HEAD_DIM, width), F32)

    def body(j, carry):
        kt = lo + 2 * j
        return step(kt + 1, 1, step(kt, 0, carry))

    init = (jnp.full((1, width), M_INIT, F32), jnp.zeros((1, width), F32),
            jnp.ones((1, width), F32))
    trips = (hi - lo + 1) // 2
    _, l, alpha = lax.fori_loop(0, trips, body, init)
    values(lo + 2 * trips - 1, 1, alpha)
    return acc_ref[...] / jnp.maximum(l, 1e-30)


def _attention_scratch(tk):
    width = N_HEADS * Q_BLOCK
    return [pltpu.VMEM((2, tk, width), F32), pltpu.VMEM((2, tk, width), BF16),
            pltpu.VMEM((HEAD_DIM, width), F32)]


def _to_token_major(oT):
    stacked = jnp.concatenate(
        [oT[:, Q_BLOCK * hh:Q_BLOCK * (hh + 1)] for hh in range(N_HEADS)], axis=0)
    return stacked.T


def _dsa_kernel(qaT_ref, qiT_ref, smallT_ref, ga_ref, knat_ref, vT_ref, out_ref,
                keys_ref, j_ref, lg_ref, s_ref, p_ref, acc_ref, *, tk, ts, topk, jbits):
    i = pl.program_id(1)
    q0 = i * Q_BLOCK
    n_kt = (q0 + Q_BLOCK + tk - 1) // tk
    n_st = (q0 + Q_BLOCK + ts - 1) // ts
    t_row = q0 + lax.broadcasted_iota(I32, (1, Q_BLOCK), 1)
    row_iota = lax.broadcasted_iota(I32, (tk, Q_BLOCK), 0)
    row_iota_s = lax.broadcasted_iota(I32, (ts, Q_BLOCK), 0)

    qi_cat = jnp.concatenate(
        [qiT_ref[0, IDX_DIM * hh:IDX_DIM * (hh + 1), :] for hh in range(IDX_HEADS)], axis=1)
    wi = smallT_ref[0, 0:IDX_HEADS, :]

    def tile_start(kt):
        return pl.multiple_of(jnp.minimum(kt, n_kt - 1) * tk, tk)

    def logits(kt):
        return _dot(knat_ref[0, pl.ds(tile_start(kt), tk), KI_OFF:KI_OFF + IDX_DIM], qi_cat)

    lg_ref[0] = logits(0)

    def score_step(kt, slot):
        k0 = tile_start(kt)
        lg = lg_ref[slot]
        lg_ref[1 - slot] = logits(kt + 1)
        sc = jnp.zeros((tk, Q_BLOCK), F32)
        for hh in range(IDX_HEADS):
            sc = sc + jnp.maximum(lg[:, Q_BLOCK * hh:Q_BLOCK * (hh + 1)], 0.0) * wi[hh:hh + 1, :]
        sc = jnp.where(sc == 0.0, 0.0, sc)
        bits = lax.bitcast_convert_type(sc, I32)
        key = bits ^ ((bits >> 31) & 0x7FFFFFFF)
        keys_ref[pl.ds(k0, tk), :] = jnp.where(k0 + row_iota <= t_row, key, INT_MIN)

    def score_body(j, carry):
        score_step(2 * j, 0)
        score_step(2 * j + 1, 1)
        return carry

    lax.fori_loop(0, (n_kt + 1) // 2, score_body, 0)

    def pad_body(kt, carry):
        keys_ref[pl.ds(pl.multiple_of(kt * tk, tk), tk), :] = jnp.full((tk, Q_BLOCK), INT_MIN, I32)
        return carry

    lax.fori_loop(n_kt, n_st * (ts // tk), pad_body, 0)

    def count(pred):
        def body(st, acc):
            k0 = pl.multiple_of(st * ts, ts)
            hit = pred(keys_ref[pl.ds(k0, ts), :], k0)
            return acc + jnp.sum(hit.reshape(ts // 8, 8, Q_BLOCK), axis=0)
        acc = lax.fori_loop(0, n_st, body, jnp.zeros((8, Q_BLOCK), I32))
        return jnp.sum(acc, axis=0, keepdims=True)

    def thr_body(it, prefix):
        cand = prefix ^ lax.shift_left(jnp.int32(1), 31 - it)
        cnt = count(lambda blk, k0: jnp.where(blk >= cand, 1, 0))
        return jnp.where(cnt >= topk, cand, prefix)

    thr = lax.fori_loop(0, 32, thr_body, jnp.full((1, Q_BLOCK), INT_MIN, I32))

    cnt_ge = count(lambda blk, k0: jnp.where(blk >= thr, 1, 0))
    j_ref[...] = jnp.full((1, Q_BLOCK), 2 ** 30, I32)

    @pl.when(jnp.max(cnt_ge) > topk)
    def _():
        need = topk - count(lambda blk, k0: jnp.where(blk > thr, 1, 0))

        def j_body(it, jp):
            cand = jp | lax.shift_left(jnp.int32(1), jbits - 1 - it)
            before = count(lambda blk, k0: jnp.where(
                blk == thr, jnp.where(k0 + row_iota_s < cand, 1, 0), 0))
            return jnp.where(before < need, cand, jp)

        j_ref[...] = lax.fori_loop(0, jbits, j_body, jnp.zeros((1, Q_BLOCK), I32))

    j_eff = jnp.minimum(j_ref[...], t_row)

    def bias_fn(k0):
        key = keys_ref[pl.ds(k0, tk), :]
        return jnp.where(key > thr, 0.0,
                         jnp.where(key == thr, jnp.where(k0 + row_iota <= j_eff, 0.0, NEG), NEG))

    oT = _masked_attention(
        _heads_on_lanes(qaT_ref),
        lambda k0: knat_ref[0, pl.ds(k0, tk), 0:HEAD_DIM],
        lambda k0: vT_ref[0, 0:HEAD_DIM, pl.ds(k0, tk)],
        bias_fn, 0, n_kt, tk, s_ref, p_ref, acc_ref)
    out_ref[0] = (_to_token_major(oT) * ga_ref[0]).astype(BF16)


def _dsa_call(qaT, qiT, smallT, gates, knat, vT, *, tk):
    B, _, T = qaT.shape
    qblk = lambda r: pl.BlockSpec((1, r, Q_BLOCK), lambda b, i: (b, 0, i))
    return pl.pallas_call(
        functools.partial(_dsa_kernel, tk=tk, ts=min(2 * tk, T), topk=min(DSA_TOPK, T // 4),
                          jbits=T.bit_length()),
        grid=(B, T // Q_BLOCK),
        in_specs=[qblk(256), qblk(256), qblk(64),
                  pl.BlockSpec((1, Q_BLOCK, 256), lambda b, i: (b, i, 0)),
                  pl.BlockSpec((1, T, 256), lambda b, i: (b, 0, 0)),
                  pl.BlockSpec((1, 192, T), lambda b, i: (b, 0, 0))],
        out_specs=pl.BlockSpec((1, Q_BLOCK, 256), lambda b, i: (b, i, 0)),
        out_shape=jax.ShapeDtypeStruct((B, T, 256), BF16),
        scratch_shapes=[pltpu.VMEM((T, Q_BLOCK), I32), pltpu.VMEM((1, Q_BLOCK), I32),
                        pltpu.VMEM((2, tk, IDX_HEADS * Q_BLOCK), F32)] + _attention_scratch(tk),
        compiler_params=pltpu.CompilerParams(
            dimension_semantics=("parallel", "arbitrary"), vmem_limit_bytes=VMEM_LIMIT),
        name="dsa",
    )(qaT, qiT, smallT, gates, knat, vT)


def _nsa_kernel(qbT_ref, smallT_ref, gb_ref, knat_ref, vT_ref, kcmp_ref, vcmpT_ref, ovT_ref,
                out_ref, sel_ref, s_ref, p_ref, acc_ref, sw_ref, pw_ref, accw_ref, *, tk, seq):
    i = pl.program_id(1)
    q0 = i * Q_BLOCK
    ncp = seq // CMP_STRIDE
    nslc = seq // SLC_BLOCK
    topn = min(SLC_TOPN, nslc)
    t_row = q0 + lax.broadcasted_iota(I32, (1, Q_BLOCK), 1)
    qT = _heads_on_lanes(qbT_ref)

    s_c = _dot(kcmp_ref[0], qT)
    n_iota = lax.broadcasted_iota(I32, (ncp, Q_BLOCK), 0)
    cend = n_iota * CMP_STRIDE + (CMP_LEN - 1)
    sm = s_c + _rep_heads(jnp.where(cend <= t_row, 0.0, NEG))
    p_c = jnp.exp2(sm - jnp.maximum(jnp.max(sm, axis=0, keepdims=True), M_INIT))
    p_c = p_c / jnp.maximum(jnp.sum(p_c, axis=0, keepdims=True), 1e-30)
    o_cmp = _dot(vcmpT_ref[0], p_c.astype(BF16))

    psum = p_c[:, 0:Q_BLOCK]
    for hh in range(1, N_HEADS):
        psum = psum + p_c[:, Q_BLOCK * hh:Q_BLOCK * (hh + 1)]
    imp = jnp.dot(ovT_ref[...], psum, preferred_element_type=F32,
                  precision=lax.Precision.HIGHEST)
    jidx = lax.broadcasted_iota(I32, (nslc, Q_BLOCK), 0)
    blk_t = t_row // SLC_BLOCK
    v = jnp.where(jidx == 0, jnp.inf,
                  jnp.where(jidx == blk_t, jnp.inf, jnp.where(jidx <= blk_t, imp, -jnp.inf)))
    sub = lax.broadcasted_iota(I32, (8, Q_BLOCK), 0)
    vg = [v[8 * g:8 * (g + 1), :] for g in range(nslc // 8)]
    rank = [jnp.zeros((8, Q_BLOCK), I32) for _ in vg]
    for jp in range(nslc):
        r = v[jp:jp + 1, :]
        for g in range(len(vg)):
            if g > jp // 8:
                beats = jnp.where(r >= vg[g], 1, 0)
            elif g < jp // 8:
                beats = jnp.where(r > vg[g], 1, 0)
            else:
                beats = jnp.where(r > vg[g], 1,
                                  jnp.where(r == vg[g], jnp.where(sub > jp % 8, 1, 0), 0))
            rank[g] = rank[g] + beats
    rank = jnp.concatenate(rank, axis=0)
    sel_ref[...] = jnp.where(rank < topn, jnp.where(jidx <= blk_t, 0.0, NEG), NEG)

    row_s = lax.broadcasted_iota(I32, (tk, Q_BLOCK), 0)

    def slc_bias(k0):
        j0 = k0 // SLC_BLOCK
        rows = [jnp.broadcast_to(sel_ref[pl.ds(j0 + jj, 1), :], (SLC_BLOCK, Q_BLOCK))
                for jj in range(tk // SLC_BLOCK)]
        return jnp.where(k0 + row_s <= t_row, jnp.concatenate(rows, axis=0), NEG)

    o_slc = _masked_attention(
        qT,
        lambda k0: knat_ref[0, pl.ds(k0, tk), HEAD_DIM:2 * HEAD_DIM],
        lambda k0: vT_ref[0, HEAD_DIM:2 * HEAD_DIM, pl.ds(k0, tk)],
        slc_bias, 0, (q0 + Q_BLOCK + tk - 1) // tk, tk, s_ref, p_ref, acc_ref)

    tw = Q_BLOCK
    row_w = lax.broadcasted_iota(I32, (tw, Q_BLOCK), 0)

    def win_bias(k0):
        pos = k0 + row_w
        return jnp.where(pos <= t_row, jnp.where(t_row - pos < WINDOW, 0.0, NEG), NEG)

    o_win = _masked_attention(
        qT,
        lambda k0: knat_ref[0, pl.ds(k0, tw), 2 * HEAD_DIM:3 * HEAD_DIM],
        lambda k0: vT_ref[0, 2 * HEAD_DIM:3 * HEAD_DIM, pl.ds(k0, tw)],
        win_bias, jnp.maximum(i - WINDOW // tw, 0), i + 1, tw, sw_ref, pw_ref, accw_ref)

    g = 1.0 / (1.0 + jnp.exp(-smallT_ref[0, IDX_HEADS:IDX_HEADS + 3 * N_HEADS, :]))
    parts = []
    for hh in range(N_HEADS):
        cols = slice(Q_BLOCK * hh, Q_BLOCK * (hh + 1))
        parts.append(g[3 * hh:3 * hh + 1, :] * o_cmp[:, cols]
                     + g[3 * hh + 1:3 * hh + 2, :] * o_slc[:, cols]
                     + g[3 * hh + 2:3 * hh + 3, :] * o_win[:, cols])
    o_tok = jnp.concatenate(parts, axis=0).T
    out_ref[0] = (o_tok * gb_ref[0]).astype(BF16)


def _nsa_call(qbT, smallT, gates, knat, vT, kcmp, vcmpT, ovT, *, tk):
    B, _, T = qbT.shape
    ncp, nslc = T // CMP_STRIDE, T // SLC_BLOCK
    qblk = lambda r: pl.BlockSpec((1, r, Q_BLOCK), lambda b, i: (b, 0, i))
    return pl.pallas_call(
        functools.partial(_nsa_kernel, tk=tk, seq=T),
        grid=(B, T // Q_BLOCK),
        in_specs=[qblk(256), qblk(64),
                  pl.BlockSpec((1, Q_BLOCK, 256), lambda b, i: (b, i, 1)),
                  pl.BlockSpec((1, T, 256), lambda b, i: (b, 0, 0)),
                  pl.BlockSpec((1, 192, T), lambda b, i: (b, 0, 0)),
                  pl.BlockSpec((1, ncp, HEAD_DIM), lambda b, i: (b, 0, 0)),
                  pl.BlockSpec((1, HEAD_DIM, ncp), lambda b, i: (b, 0, 0)),
                  pl.BlockSpec((nslc, ncp), lambda b, i: (0, 0))],
        out_specs=pl.BlockSpec((1, Q_BLOCK, 256), lambda b, i: (b, i, 0)),
        out_shape=jax.ShapeDtypeStruct((B, T, 256), BF16),
        scratch_shapes=[pltpu.VMEM((nslc, Q_BLOCK), F32)] + _attention_scratch(tk)
        + _attention_scratch(Q_BLOCK),
        compiler_params=pltpu.CompilerParams(
            dimension_semantics=("parallel", "arbitrary"), vmem_limit_bytes=VMEM_LIMIT),
        name="nsa",
    )(qbT, smallT, gates, knat, vT, kcmp, vcmpT, ovT)


def _out_kernel(x_ref, ma_ref, mb_ref, mcd_ref, wo_ref, fw_ref, o_ref, *, final):
    y = (x_ref[...] + _dot(ma_ref[...], wo_ref[0:256, :]) + _dot(mb_ref[...], wo_ref[256:512, :])
         + _dot(mcd_ref[...], wo_ref[512:1024, :]))
    if final:
        ms = jnp.mean(y * y, axis=-1, keepdims=True)
        y = y * lax.rsqrt(ms + EPS) * fw_ref[...]
    o_ref[...] = y


def _out_call(x2, ma, mb, mcd, wo, fw, *, final, tm):
    R, D = x2.shape
    rows = lambda w: pl.BlockSpec((tm, w), lambda r: (r, 0))
    return pl.pallas_call(
        functools.partial(_out_kernel, final=final),
        grid=(R // tm,),
        in_specs=[rows(D), rows(256), rows(256), rows(512),
                  pl.BlockSpec((D, D), lambda r: (0, 0)), pl.BlockSpec((1, D), lambda r: (0, 0))],
        out_specs=rows(D),
        out_shape=jax.ShapeDtypeStruct((R, D), F32),
        compiler_params=pltpu.CompilerParams(
            dimension_semantics=("parallel",), vmem_limit_bytes=VMEM_LIMIT),
        name="outproj",
    )(x2, ma, mb, mcd, wo, fw)


def _rope_angles(pos, rot):
    half = rot // 2
    inv = ROPE_THETA ** (-jnp.arange(half, dtype=F32) / half)
    ang = pos.astype(F32)[:, None] * inv[None, :]
    return jnp.cos(ang), jnp.sin(ang)


def _rope_tables(T):
    pos = jnp.arange(T)
    c16, s16 = _rope_angles(pos, ROT_DIM)
    c8, s8 = _rope_angles(pos, IDX_ROT)
    ones, zeros = jnp.ones((T, 256), F32), jnp.zeros((T, 256), F32)
    C, Sa, Sb, Sc, Sd = ones, zeros, zeros, zeros, zeros
    for off in (0, 64, 128):
        C = C.at[:, off:off + 8].set(c16).at[:, off + 8:off + 16].set(c16)
        Sa = Sa.at[:, off:off + 8].set(-s16)
        Sb = Sb.at[:, off + 8:off + 16].set(s16)
    C = C.at[:, KI_OFF:KI_OFF + 4].set(c8).at[:, KI_OFF + 4:KI_OFF + 8].set(c8)
    Sc = Sc.at[:, KI_OFF:KI_OFF + 4].set(-s8)
    Sd = Sd.at[:, KI_OFF + 4:KI_OFF + 8].set(s8)
    tabn = jnp.stack([C, Sa, Sb, Sc, Sd])
    tabt = jnp.stack([c16.T, s16.T, jnp.concatenate([c8.T, c8.T]),
                      jnp.concatenate([-s8.T, s8.T])])
    ncp = T // CMP_STRIDE
    cc, sc_ = _rope_angles(jnp.arange(ncp) * CMP_STRIDE + CMP_LEN - 1, ROT_DIM)
    Cc = jnp.ones((ncp, 128), F32).at[:, 0:8].set(cc).at[:, 8:16].set(cc)
    Sac = jnp.zeros((ncp, 128), F32).at[:, 0:8].set(-sc_)
    Sbc = jnp.zeros((ncp, 128), F32).at[:, 8:16].set(sc_)
    return tabn, tabt, jnp.stack([Cc, Sac, Sbc])


def _split_offsets():
    widths = (("a_q", 256), ("a_k", 64), ("a_v", 64), ("a_qi", 256), ("a_ki", 32), ("a_wi", 8),
              ("a_gate", 256), ("b_q", 256), ("b_kc", 64), ("b_vc", 64), ("b_ks", 64),
              ("b_vs", 64), ("b_kw", 64), ("b_vw", 64), ("b_g", 12), ("b_gate", 256),
              ("c_b", 256), ("c_c", 256), ("c_x", 256), ("c_gate", 256), ("d_u", 256),
              ("d_gate", 256))
    offs, o = {}, 0
    for name, w in widths:
        offs[name] = (o, o + w)
        o += w
    return offs


def _layer_weights(w_in, pool_w, pe_cmp, w_cmp_k, w_cmp_v):
    offs = _split_offsets()
    col = lambda n: w_in[:, offs[n][0]:offs[n][1]]
    D = w_in.shape[0]
    wn = jnp.concatenate(
        [col("a_k"), col("b_ks"), col("b_kw"), col("a_ki"), jnp.zeros((D, 32), F32),
         col("b_kc"), col("b_vc"), col("a_gate"), col("b_gate"), col("c_b"), col("c_c"),
         col("c_x"), col("c_gate"), col("d_u"), col("d_gate")], axis=1).astype(BF16)
    qscale = HEAD_DIM ** -0.5 * LOG2E
    iscale = (IDX_DIM ** -0.5) * (IDX_HEADS ** -0.5)
    wt = jnp.concatenate(
        [col("a_q") * qscale, col("a_qi"), col("b_q") * qscale, col("a_v"), col("b_vs"),
         col("b_vw"), col("a_wi") * iscale, col("b_g"), jnp.zeros((D, 44), F32)],
        axis=1).T.astype(BF16)
    groups = pool_w.shape[0]
    pch = pool_w.shape[1]
    poolw = jnp.zeros((groups * pch, groups * pch), F32)
    for g in range(groups):
        poolw = poolw.at[g * pch:(g + 1) * pch, g * pch:(g + 1) * pch].set(pool_w[g])
    half = CMP_LEN // 2
    wk3 = w_cmp_k.reshape(CMP_LEN, HEAD_DIM, HEAD_DIM)
    wv3 = w_cmp_v.reshape(CMP_LEN, HEAD_DIM, HEAD_DIM)

    def chunk_w(lo):
        blk = jnp.zeros((half, 128, 128), F32)
        blk = blk.at[:, 0:64, 0:64].set(wk3[lo:lo + half]).at[:, 64:128, 64:128].set(wv3[lo:lo + half])
        return blk.reshape(half * 128, 128)

    wc = jnp.stack([chunk_w(0), chunk_w(half)]).astype(BF16)
    pe_rows = lambda lo: jnp.concatenate([pe_cmp[lo:lo + half]] * 2, axis=1).reshape(1, half * 128)
    pe2 = jnp.stack([pe_rows(0), pe_rows(half)])
    return wn, wt, poolw.astype(BF16), wc, pe2


def _overlap_T(T):
    ncp, nslc = T // CMP_STRIDE, T // SLC_BLOCK
    n_cmp = (T - CMP_LEN) // CMP_STRIDE + 1
    cs = np.arange(ncp) * CMP_STRIDE
    ce = cs + CMP_LEN - 1
    ss = np.arange(nslc) * SLC_BLOCK
    ov = (cs[None, :] < ss[:, None] + SLC_BLOCK) & (ce[None, :] >= ss[:, None]) & (np.arange(ncp)[None, :] < n_cmp)
    return jnp.asarray(ov.astype(np.float32))


def kernel(x, norm_w, w_in, w_out, conv_w, pe_cmp, w_cmp_k, w_cmp_v, pool_w, pool_scale, final_norm_w):
    B, T, D = x.shape
    depth = w_in.shape[0]
    tm = min(512, T)
    tk = min(256, T)
    tabn, tabt, tabc = _rope_tables(T)
    ovT = _overlap_T(T)
    fw = final_norm_w.reshape(1, D)
    for l in range(depth):
        wn, wt, poolw, wc, pe2 = _layer_weights(w_in[l], pool_w[l], pe_cmp[l], w_cmp_k[l], w_cmp_v[l])
        knat, kvc, gates, mcd, qaT, qiT, qbT, vT, smallT = _proj_call(
            x, norm_w[l].reshape(1, D), wn, wt, tabn, tabt, conv_w[l], poolw,
            pool_scale[l].reshape(1, W_GROUP), tm=tm)
        kcmp, vcmpT = _cmp_call(kvc.reshape(B, T // CMP_STRIDE, CMP_STRIDE * 128), pe2, wc, tabc)
        ma = _dsa_call(qaT, qiT, smallT, gates, knat, vT, tk=tk)
        mb = _nsa_call(qbT, smallT, gates, knat, vT, kcmp, vcmpT, ovT, tk=tk)
        x = _out_call(x.reshape(B * T, D), ma.reshape(B * T, 256), mb.reshape(B * T, 256),
                      mcd.reshape(B * T, 512), w_out[l].astype(BF16), fw,
                      final=(l == depth - 1), tm=tm).reshape(B, T, D)
    return x
```

```python
import functools

import numpy as np
import jax
import jax.numpy as jnp
from jax import lax
from jax.experimental import pallas as pl
from jax.experimental.pallas import tpu as pltpu

HEAD_DIM = 64
N_HEADS = 4
ROT_DIM = HEAD_DIM // 4
ROPE_THETA = 500000.0
EPS = 1e-6
IDX_HEADS = 8
IDX_DIM = 32
IDX_ROT = IDX_DIM // 4
DSA_TOPK = 256
CMP_LEN = 32
CMP_STRIDE = 16
SLC_BLOCK = 64
SLC_TOPN = 16
WINDOW = 512
CONV_WIDTH = 3
POOL_WINDOWS = (2, 4, 8, 16)
W_GROUP = 256

Q_BLOCK = 128
LANES = 128
HALO = 16
NEG = -1e30
M_INIT = -1e29
LOG2E = 1.4426950408889634
VMEM_LIMIT = 48 * 1024 * 1024

F32 = jnp.float32
BF16 = jnp.bfloat16
I32 = jnp.int32
INT_MIN = -2 ** 31

N_K = 0
N_KVC = 256
N_GATE = 384
N_CB, N_CC, N_CX, N_CG, N_DU, N_DG = 896, 1152, 1408, 1664, 1920, 2176
N_NAT = 2432
T_QA, T_QI, T_QB, T_V, T_SMALL, N_TR = 0, 256, 512, 768, 960, 1024
KI_OFF = 192


def _silu(v):
    return v / (1.0 + jnp.exp(-v))


def _dot(a, b):
    return jnp.dot(a, b, preferred_element_type=F32)


def _proj_kernel(x_ref, nw_ref, wn_ref, wt_ref, tabn_ref, tabt_ref, convw_ref, poolw_ref,
                 pscale_ref, knat_o, kvc_o, gates_o, mcd_o, qaT_o, qiT_o, qbT_o, vT_o, smallT_o,
                 cu_ref, du_ref, *, tm):
    t_idx = pl.program_id(1)
    x = x_ref[0]
    ms = jnp.mean(x * x, axis=-1, keepdims=True)
    h = (x * lax.rsqrt(ms + EPS) * nw_ref[...]).astype(BF16)
    zn = _dot(h, wn_ref[...])
    zt = lax.dot_general(wt_ref[...], h, (((1,), (1,)), ((), ())),
                         preferred_element_type=F32)

    zk = zn[:, N_K:N_K + 256]
    kr = (zk * tabn_ref[0]
          + pltpu.roll(zk, 256 - 8, 1) * tabn_ref[1] + pltpu.roll(zk, 8, 1) * tabn_ref[2]
          + pltpu.roll(zk, 256 - 4, 1) * tabn_ref[3] + pltpu.roll(zk, 4, 1) * tabn_ref[4])
    knat_o[0] = kr.astype(BF16)
    kvc_o[0] = zn[:, N_KVC:N_KVC + 128]
    gates_o[0] = _silu(zn[:, N_GATE:N_GATE + 512])

    c_b = zn[:, N_CB:N_CB + 256]
    u = zn[:, N_CC:N_CC + 256] * zn[:, N_CX:N_CX + 256]
    d_u = zn[:, N_DU:N_DU + 256]

    @pl.when(t_idx == 0)
    def _():
        cu_ref[0:HALO, :] = jnp.zeros((HALO, 256), F32)
        du_ref[0:HALO, :] = jnp.zeros((HALO, 256), F32)

    @pl.when(t_idx > 0)
    def _():
        cu_ref[0:HALO, :] = cu_ref[tm:tm + HALO, :]
        du_ref[0:HALO, :] = du_ref[tm:tm + HALO, :]

    cu_ref[HALO:HALO + tm, :] = u
    du_ref[HALO:HALO + tm, :] = d_u
    cw = convw_ref[...]
    y = (cu_ref[HALO - 2:HALO - 2 + tm, :] * cw[0:1, :]
         + cu_ref[HALO - 1:HALO - 1 + tm, :] * cw[1:2, :] + u * cw[2:3, :])
    o_c = c_b * y

    acc = d_u
    sums = {}
    for k in range(1, max(POOL_WINDOWS)):
        acc = acc + du_ref[HALO - k:HALO - k + tm, :]
        if k + 1 in POOL_WINDOWS:
            sums[k + 1] = acc
    lane = lax.broadcasted_iota(I32, (tm, 256), 1)
    row = lax.broadcasted_iota(I32, (tm, 256), 0)
    pch = W_GROUP // len(POOL_WINDOWS)
    ssel = sums[POOL_WINDOWS[-1]]
    wl = jnp.full((tm, 256), POOL_WINDOWS[-1], I32)
    for g in range(len(POOL_WINDOWS) - 2, -1, -1):
        ssel = jnp.where(lane < (g + 1) * pch, sums[POOL_WINDOWS[g]], ssel)
        wl = jnp.where(lane < (g + 1) * pch, POOL_WINDOWS[g], wl)
    cnt = jnp.minimum(t_idx * tm + row + 1, wl).astype(F32)
    pooled = ssel / cnt - d_u
    o_d = _dot(pooled.astype(BF16), poolw_ref[...]) * pscale_ref[...]
    mcd_o[0] = jnp.concatenate(
        [_silu(zn[:, N_CG:N_CG + 256]) * o_c, _silu(zn[:, N_DG:N_DG + 256]) * o_d],
        axis=1).astype(BF16)

    c16, s16, c8, s8 = tabt_ref[0], tabt_ref[1], tabt_ref[2], tabt_ref[3]

    def rope_heads(base, out_ref):
        for hh in range(N_HEADS):
            b = base + HEAD_DIM * hh
            x1, x2 = zt[b:b + 8, :], zt[b + 8:b + 16, :]
            o = jnp.concatenate([x1 * c16 - x2 * s16, x2 * c16 + x1 * s16], axis=0)
            out_ref[0, HEAD_DIM * hh:HEAD_DIM * hh + 16, :] = o.astype(BF16)
            out_ref[0, HEAD_DIM * hh + 16:HEAD_DIM * (hh + 1), :] = zt[b + 16:b + HEAD_DIM, :].astype(BF16)

    rope_heads(T_QA, qaT_o)
    rope_heads(T_QB, qbT_o)
    for hh in range(IDX_HEADS):
        b = T_QI + IDX_DIM * hh
        x8 = zt[b:b + 8, :]
        o = x8 * c8 + pltpu.roll(x8, 4, 0) * s8
        qiT_o[0, IDX_DIM * hh:IDX_DIM * hh + 16, :] = jnp.concatenate(
            [o, zt[b + 8:b + 16, :]], axis=0).astype(BF16)
        qiT_o[0, IDX_DIM * hh + 16:IDX_DIM * (hh + 1), :] = zt[b + 16:b + IDX_DIM, :].astype(BF16)
    vT_o[0] = zt[T_V:T_V + 192, :].astype(BF16)
    smallT_o[0] = zt[T_SMALL:T_SMALL + 64, :]


def _proj_call(x, nw, wn, wt, tabn, tabt, convw, poolw, pscale, *, tm):
    B, T, D = x.shape
    grid = (B, T // tm)
    const = lambda *shape: pl.BlockSpec(shape, lambda b, t: (0,) * len(shape))
    nat = lambda w: pl.BlockSpec((1, tm, w), lambda b, t: (b, t, 0))
    tr = lambda r: pl.BlockSpec((1, r, tm), lambda b, t: (b, 0, t))
    out_shape = (
        jax.ShapeDtypeStruct((B, T, 256), BF16), jax.ShapeDtypeStruct((B, T, 128), F32),
        jax.ShapeDtypeStruct((B, T, 512), F32), jax.ShapeDtypeStruct((B, T, 512), BF16),
        jax.ShapeDtypeStruct((B, 256, T), BF16), jax.ShapeDtypeStruct((B, 256, T), BF16),
        jax.ShapeDtypeStruct((B, 256, T), BF16), jax.ShapeDtypeStruct((B, 192, T), BF16),
        jax.ShapeDtypeStruct((B, 64, T), F32))
    return pl.pallas_call(
        functools.partial(_proj_kernel, tm=tm),
        grid=grid,
        in_specs=[nat(D), const(1, D), const(D, N_NAT), const(N_TR, D),
                  pl.BlockSpec((5, tm, 256), lambda b, t: (0, t, 0)),
                  pl.BlockSpec((4, 8, tm), lambda b, t: (0, 0, t)),
                  const(CONV_WIDTH, 256), const(256, 256), const(1, 256)],
        out_specs=(nat(256), nat(128), nat(512), nat(512), tr(256), tr(256), tr(256), tr(192), tr(64)),
        out_shape=out_shape,
        scratch_shapes=[pltpu.VMEM((tm + HALO, 256), F32), pltpu.VMEM((tm + HALO, 256), F32)],
        compiler_params=pltpu.CompilerParams(
            dimension_semantics=("parallel", "arbitrary"), vmem_limit_bytes=VMEM_LIMIT),
        name="proj",
    )(x, nw, wn, wt, tabn, tabt, convw, poolw, pscale)


def _cmp_kernel(kvc_ref, pe_ref, wc_ref, tab_ref, kcmp_o, vcmpT_o, *, ncp):
    c = kvc_ref[0]
    first = _dot((c + pe_ref[0]).astype(BF16), wc_ref[0])
    second = _dot((c + pe_ref[1]).astype(BF16), wc_ref[1])
    kv = first + pltpu.roll(second, ncp - 1, 0)
    kr = (kv * tab_ref[0] + pltpu.roll(kv, 128 - 8, 1) * tab_ref[1]
          + pltpu.roll(kv, 8, 1) * tab_ref[2])
    kcmp_o[0] = kr[:, 0:HEAD_DIM].astype(BF16)
    vcmpT_o[0] = kr.T[HEAD_DIM:2 * HEAD_DIM, :].astype(BF16)


def _cmp_call(kvc_chunks, pe2, wc, tabc):
    B, ncp, width = kvc_chunks.shape
    return pl.pallas_call(
        functools.partial(_cmp_kernel, ncp=ncp),
        grid=(B,),
        in_specs=[pl.BlockSpec((1, ncp, width), lambda b: (b, 0, 0)),
                  pl.BlockSpec((2, 1, width), lambda b: (0, 0, 0)),
                  pl.BlockSpec((2, width, 128), lambda b: (0, 0, 0)),
                  pl.BlockSpec((3, ncp, 128), lambda b: (0, 0, 0))],
        out_specs=(pl.BlockSpec((1, ncp, HEAD_DIM), lambda b: (b, 0, 0)),
                   pl.BlockSpec((1, HEAD_DIM, ncp), lambda b: (b, 0, 0))),
        out_shape=(jax.ShapeDtypeStruct((B, ncp, HEAD_DIM), BF16),
                   jax.ShapeDtypeStruct((B, HEAD_DIM, ncp), BF16)),
        compiler_params=pltpu.CompilerParams(
            dimension_semantics=("parallel",), vmem_limit_bytes=VMEM_LIMIT),
        name="compress",
    )(kvc_chunks, pe2, wc, tabc)


def _heads_on_lanes(qT_ref):
    return jnp.concatenate(
        [qT_ref[0, HEAD_DIM * hh:HEAD_DIM * (hh + 1), :] for hh in range(N_HEADS)], axis=1)


def _rep_heads(a):
    return jnp.concatenate([a] * N_HEADS, axis=1)


def _masked_attention(qT, k_load, vT_load, bias_fn, lo, hi, tk, s_ref, p_ref, acc_ref):
    width = N_HEADS * Q_BLOCK
    last = hi - 1

    def tile_start(kt):
        return pl.multiple_of(jnp.minimum(kt, last) * tk, tk)

    def scores(kt):
        return _dot(k_load(tile_start(kt)), qT)

    def values(kt, slot, alpha):
        acc_ref[...] = alpha * acc_ref[...] + _dot(vT_load(tile_start(kt)), p_ref[slot])

    def step(kt, slot, carry):
        m, l, alpha_prev = carry
        values(jnp.maximum(kt - 1, lo), 1 - slot, alpha_prev)
        s = s_ref[slot]
        s_ref[1 - slot] = scores(kt + 1)
        bias = jnp.where(kt <= last, bias_fn(tile_start(kt)), NEG)
        sm = s + _rep_heads(bias)
        m_new = jnp.maximum(m, jnp.max(sm, axis=0, keepdims=True))
        alpha = jnp.exp2(m - m_new)
        p = jnp.exp2(sm - m_new)
        p_ref[slot] = p.astype(BF16)
        return m_new, alpha * l + jnp.sum(p, axis=0, keepdims=True), alpha

    s_ref[0] = scores(lo)
    p_ref[1] = jnp.zeros((tk, width), BF16)
    acc_ref[...] = jnp.zeros((HEAD_DIM, width), F32)

    def body(j, carry):
        kt = lo + 2 * j
        return step(kt + 1, 1, step(kt, 0, carry))

    init = (jnp.full((1, width), M_INIT, F32), jnp.zeros((1, width), F32),
            jnp.ones((1, width), F32))
    trips = (hi - lo + 1) // 2
    _, l, alpha = lax.fori_loop(0, trips, body, init)
    values(lo + 2 * trips - 1, 1, alpha)
    return acc_ref[...] / jnp.maximum(l, 1e-30)


def _attention_scratch(tk):
    width = N_HEADS * Q_BLOCK
    return [pltpu.VMEM((2, tk, width), F32), pltpu.VMEM((2, tk, width), BF16),
            pltpu.VMEM((HEAD_DIM, width), F32)]


def _to_token_major(oT):
    stacked = jnp.concatenate(
        [oT[:, Q_BLOCK * hh:Q_BLOCK * (hh + 1)] for hh in range(N_HEADS)], axis=0)
    return stacked.T


_SWAP_MASK = {16: 0x0000FFFF, 8: 0x00FF00FF, 4: 0x0F0F0F0F, 2: 0x33333333, 1: 0x55555555}


def _bit_swap(words, lo, j):
    t = (words[lo] ^ (words[lo + j] >> j)) & _SWAP_MASK[j]
    words[lo] = words[lo] ^ t
    words[lo + j] = words[lo + j] ^ (t << j)


def _dsa_kernel(qaT_ref, qiT_ref, smallT_ref, ga_ref, knat_ref, vT_ref, out_ref,
                keys_ref, planes_ref, j_ref, lg_ref, s_ref, p_ref, acc_ref, *, tk, ts, topk, jbits):
    i = pl.program_id(1)

    @pl.when((pl.program_id(0) == 0) & (i == 0))
    def _():
        planes_ref[...] = jnp.zeros(planes_ref.shape, I32)

    q0 = i * Q_BLOCK
    n_kt = (q0 + Q_BLOCK + tk - 1) // tk
    n_st = (q0 + Q_BLOCK + ts - 1) // ts
    t_row = q0 + lax.broadcasted_iota(I32, (1, Q_BLOCK), 1)
    row_iota = lax.broadcasted_iota(I32, (tk, Q_BLOCK), 0)
    row_iota_s = lax.broadcasted_iota(I32, (ts, Q_BLOCK), 0)

    qi_cat = jnp.concatenate(
        [qiT_ref[0, IDX_DIM * hh:IDX_DIM * (hh + 1), :] for hh in range(IDX_HEADS)], axis=1)
    wi = smallT_ref[0, 0:IDX_HEADS, :]

    def tile_start(kt):
        return pl.multiple_of(jnp.minimum(kt, n_kt - 1) * tk, tk)

    def logits(kt):
        return _dot(knat_ref[0, pl.ds(tile_start(kt), tk), KI_OFF:KI_OFF + IDX_DIM], qi_cat)

    lg_ref[0] = logits(0)

    wi_rows = [jnp.broadcast_to(wi[hh:hh + 1, :], (8, Q_BLOCK)) for hh in range(IDX_HEADS)]
    row8 = lax.broadcasted_iota(I32, (8, Q_BLOCK), 0)

    def slab_key(slot, k0, a):
        sc = jnp.zeros((8, Q_BLOCK), F32)
        for hh in range(IDX_HEADS):
            lg = lg_ref[slot, 8 * a:8 * (a + 1), Q_BLOCK * hh:Q_BLOCK * (hh + 1)]
            sc = sc + jnp.maximum(lg, 0.0) * wi_rows[hh]
        sc = jnp.where(sc == 0.0, 0.0, sc)
        bits = lax.bitcast_convert_type(sc, I32)
        key = bits ^ ((bits >> 31) & 0x7FFFFFFF)
        key = jnp.where(k0 + 8 * a + row8 <= t_row, key, INT_MIN)
        keys_ref[pl.ds(k0 + 8 * a, 8), :] = key
        return key

    def score_step(kt, slot):
        k0 = tile_start(kt)
        tile = k0 // tk
        lg_ref[1 - slot] = logits(kt + 1)
        for q in range(8):
            words = {a: slab_key(slot, k0, a) ^ INT_MIN for a in (q, q + 8, q + 16, q + 24)}
            _bit_swap(words, q, 16)
            _bit_swap(words, q + 8, 16)
            _bit_swap(words, q, 8)
            _bit_swap(words, q + 16, 8)
            for a, w in words.items():
                planes_ref[tile, a] = w
        for g in range(0, 32, 8):
            words = {a: planes_ref[tile, a] for a in range(g, g + 8)}
            for j in (4, 2, 1):
                for lo in range(g, g + 8):
                    if not lo & j:
                        _bit_swap(words, lo, j)
            for a, w in words.items():
                planes_ref[tile, a] = w

    def score_body(j, carry):
        score_step(2 * j, 0)
        score_step(2 * j + 1, 1)
        return carry

    lax.fori_loop(0, (n_kt + 1) // 2, score_body, 0)

    n_tiles = planes_ref.shape[0]

    def bit_step(b, carry):
        alive, c_above, thr_u = carry
        ones = [alive[a] & planes_ref[a, b] for a in range(n_tiles)]
        pcs = [lax.population_count(o) for o in ones]
        while len(pcs) > 1:
            pcs = [pcs[a] + pcs[a + 1] for a in range(0, len(pcs), 2)]
        cnt1 = jnp.sum(pcs[0], axis=0, keepdims=True)
        take1 = c_above + cnt1 >= topk
        alive = tuple(jnp.where(take1, o, al ^ o) for al, o in zip(alive, ones))
        bit = lax.shift_left(jnp.int32(1), 31 - b)
        return alive, jnp.where(take1, c_above, c_above + cnt1), jnp.where(take1, thr_u | bit, thr_u)

    alive0 = tuple(jnp.where(a < n_kt, jnp.full((8, Q_BLOCK), -1, I32), 0) for a in range(n_tiles))
    zero_row = jnp.zeros((1, Q_BLOCK), I32)
    alive, c_above, thr_u = lax.fori_loop(0, 32, bit_step, (alive0, zero_row, zero_row))
    thr = thr_u ^ INT_MIN
    n_eq = [lax.population_count(al) for al in alive]
    while len(n_eq) > 1:
        n_eq = [n_eq[a] + n_eq[a + 1] for a in range(0, len(n_eq), 2)]
    cnt_ge = c_above + jnp.sum(n_eq[0], axis=0, keepdims=True)

    def pad_body(kt, carry):
        keys_ref[pl.ds(pl.multiple_of(kt * tk, tk), tk), :] = jnp.full((tk, Q_BLOCK), INT_MIN, I32)
        return carry

    def count(pred):
        def body(st, acc):
            k0 = pl.multiple_of(st * ts, ts)
            hit = pred(keys_ref[pl.ds(k0, ts), :], k0)
            return acc + jnp.sum(hit.reshape(ts // 8, 8, Q_BLOCK), axis=0)
        acc = lax.fori_loop(0, n_st, body, jnp.zeros((8, Q_BLOCK), I32))
        return jnp.sum(acc, axis=0, keepdims=True)

    j_ref[...] = jnp.full((1, Q_BLOCK), 2 ** 30, I32)

    @pl.when(jnp.max(cnt_ge) > topk)
    def _():
        lax.fori_loop(n_kt, n_st * (ts // tk), pad_body, 0)
        need = topk - c_above

        def j_body(it, jp):
            cand = jp | lax.shift_left(jnp.int32(1), jbits - 1 - it)
            before = count(lambda blk, k0: jnp.where(
                blk == thr, jnp.where(k0 + row_iota_s < cand, 1, 0), 0))
            return jnp.where(before < need, cand, jp)

        j_ref[...] = lax.fori_loop(0, jbits, j_body, jnp.zeros((1, Q_BLOCK), I32))

    j_eff = jnp.minimum(j_ref[...], t_row)

    def bias_fn(k0):
        key = keys_ref[pl.ds(k0, tk), :]
        return jnp.where(key > thr, 0.0,
                         jnp.where(key == thr, jnp.where(k0 + row_iota <= j_eff, 0.0, NEG), NEG))

    oT = _masked_attention(
        _heads_on_lanes(qaT_ref),
        lambda k0: knat_ref[0, pl.ds(k0, tk), 0:HEAD_DIM],
        lambda k0: vT_ref[0, 0:HEAD_DIM, pl.ds(k0, tk)],
        bias_fn, 0, n_kt, tk, s_ref, p_ref, acc_ref)
    out_ref[0] = (_to_token_major(oT) * ga_ref[0]).astype(BF16)


def _dsa_call(qaT, qiT, smallT, gates, knat, vT, *, tk):
    B, _, T = qaT.shape
    assert tk == 32 * 8, "a key tile is transposed as 32 vregs of 8 rows"
    qblk = lambda r: pl.BlockSpec((1, r, Q_BLOCK), lambda b, i: (b, 0, i))
    return pl.pallas_call(
        functools.partial(_dsa_kernel, tk=tk, ts=min(2 * tk, T), topk=min(DSA_TOPK, T // 4),
                          jbits=T.bit_length()),
        grid=(B, T // Q_BLOCK),
        in_specs=[qblk(256), qblk(256), qblk(64),
                  pl.BlockSpec((1, Q_BLOCK, 256), lambda b, i: (b, i, 0)),
                  pl.BlockSpec((1, T, 256), lambda b, i: (b, 0, 0)),
                  pl.BlockSpec((1, 192, T), lambda b, i: (b, 0, 0))],
        out_specs=pl.BlockSpec((1, Q_BLOCK, 256), lambda b, i: (b, i, 0)),
        out_shape=jax.ShapeDtypeStruct((B, T, 256), BF16),
        scratch_shapes=[pltpu.VMEM((T, Q_BLOCK), I32), pltpu.VMEM((T // tk, 32, 8, Q_BLOCK), I32),
                        pltpu.VMEM((1, Q_BLOCK), I32),
                        pltpu.VMEM((2, tk, IDX_HEADS * Q_BLOCK), F32)] + _attention_scratch(tk),
        compiler_params=pltpu.CompilerParams(
            dimension_semantics=("parallel", "arbitrary"), vmem_limit_bytes=VMEM_LIMIT),
        name="dsa",
    )(qaT, qiT, smallT, gates, knat, vT)


def _nsa_kernel(qbT_ref, smallT_ref, gb_ref, knat_ref, vT_ref, kcmp_ref, vcmpT_ref, ovT_ref,
                out_ref, sel_ref, s_ref, p_ref, acc_ref, sw_ref, pw_ref, accw_ref, *, tk, seq):
    i = pl.program_id(1)
    q0 = i * Q_BLOCK
    ncp = seq // CMP_STRIDE
    nslc = seq // SLC_BLOCK
    topn = min(SLC_TOPN, nslc)
    t_row = q0 + lax.broadcasted_iota(I32, (1, Q_BLOCK), 1)
    qT = _heads_on_lanes(qbT_ref)

    s_c = _dot(kcmp_ref[0], qT)
    n_iota = lax.broadcasted_iota(I32, (ncp, Q_BLOCK), 0)
    cend = n_iota * CMP_STRIDE + (CMP_LEN - 1)
    sm = s_c + _rep_heads(jnp.where(cend <= t_row, 0.0, NEG))
    p_c = jnp.exp2(sm - jnp.maximum(jnp.max(sm, axis=0, keepdims=True), M_INIT))
    p_c = p_c / jnp.maximum(jnp.sum(p_c, axis=0, keepdims=True), 1e-30)
    o_cmp = _dot(vcmpT_ref[0], p_c.astype(BF16))

    psum = p_c[:, 0:Q_BLOCK]
    for hh in range(1, N_HEADS):
        psum = psum + p_c[:, Q_BLOCK * hh:Q_BLOCK * (hh + 1)]
    imp = jnp.dot(ovT_ref[...], psum, preferred_element_type=F32,
                  precision=lax.Precision.HIGHEST)
    jidx = lax.broadcasted_iota(I32, (nslc, Q_BLOCK), 0)
    blk_t = t_row // SLC_BLOCK
    v = jnp.where(jidx == 0, jnp.inf,
                  jnp.where(jidx == blk_t, jnp.inf, jnp.where(jidx <= blk_t, imp, -jnp.inf)))
    sub = lax.broadcasted_iota(I32, (8, Q_BLOCK), 0)
    vg = [v[8 * g:8 * (g + 1), :] for g in range(nslc // 8)]
    rank = [jnp.zeros((8, Q_BLOCK), I32) for _ in vg]
    for jp in range(nslc):
        r = v[jp:jp + 1, :]
        for g in range(len(vg)):
            if g > jp // 8:
                beats = jnp.where(r >= vg[g], 1, 0)
            elif g < jp // 8:
                beats = jnp.where(r > vg[g], 1, 0)
            else:
                beats = jnp.where(r > vg[g], 1,
                                  jnp.where(r == vg[g], jnp.where(sub > jp % 8, 1, 0), 0))
            rank[g] = rank[g] + beats
    rank = jnp.concatenate(rank, axis=0)
    sel_ref[...] = jnp.where(rank < topn, jnp.where(jidx <= blk_t, 0.0, NEG), NEG)

    row_s = lax.broadcasted_iota(I32, (tk, Q_BLOCK), 0)

    def slc_bias(k0):
        j0 = k0 // SLC_BLOCK
        rows = [jnp.broadcast_to(sel_ref[pl.ds(j0 + jj, 1), :], (SLC_BLOCK, Q_BLOCK))
                for jj in range(tk // SLC_BLOCK)]
        return jnp.where(k0 + row_s <= t_row, jnp.concatenate(rows, axis=0), NEG)

    o_slc = _masked_attention(
        qT,
        lambda k0: knat_ref[0, pl.ds(k0, tk), HEAD_DIM:2 * HEAD_DIM],
        lambda k0: vT_ref[0, HEAD_DIM:2 * HEAD_DIM, pl.ds(k0, tk)],
        slc_bias, 0, (q0 + Q_BLOCK + tk - 1) // tk, tk, s_ref, p_ref, acc_ref)

    tw = Q_BLOCK
    row_w = lax.broadcasted_iota(I32, (tw, Q_BLOCK), 0)

    def win_bias(k0):
        pos = k0 + row_w
        return jnp.where(pos <= t_row, jnp.where(t_row - pos < WINDOW, 0.0, NEG), NEG)

    o_win = _masked_attention(
        qT,
        lambda k0: knat_ref[0, pl.ds(k0, tw), 2 * HEAD_DIM:3 * HEAD_DIM],
        lambda k0: vT_ref[0, 2 * HEAD_DIM:3 * HEAD_DIM, pl.ds(k0, tw)],
        win_bias, jnp.maximum(i - WINDOW // tw, 0), i + 1, tw, sw_ref, pw_ref, accw_ref)

    g = 1.0 / (1.0 + jnp.exp(-smallT_ref[0, IDX_HEADS:IDX_HEADS + 3 * N_HEADS, :]))
    parts = []
    for hh in range(N_HEADS):
        cols = slice(Q_BLOCK * hh, Q_BLOCK * (hh + 1))
        parts.append(g[3 * hh:3 * hh + 1, :] * o_cmp[:, cols]
                     + g[3 * hh + 1:3 * hh + 2, :] * o_slc[:, cols]
                     + g[3 * hh + 2:3 * hh + 3, :] * o_win[:, cols])
    o_tok = jnp.concatenate(parts, axis=0).T
    out_ref[0] = (o_tok * gb_ref[0]).astype(BF16)


def _nsa_call(qbT, smallT, gates, knat, vT, kcmp, vcmpT, ovT, *, tk):
    B, _, T = qbT.shape
    ncp, nslc = T // CMP_STRIDE, T // SLC_BLOCK
    qblk = lambda r: pl.BlockSpec((1, r, Q_BLOCK), lambda b, i: (b, 0, i))
    return pl.pallas_call(
        functools.partial(_nsa_kernel, tk=tk, seq=T),
        grid=(B, T // Q_BLOCK),
        in_specs=[qblk(256), qblk(64),
                  pl.BlockSpec((1, Q_BLOCK, 256), lambda b, i: (b, i, 1)),
                  pl.BlockSpec((1, T, 256), lambda b, i: (b, 0, 0)),
                  pl.BlockSpec((1, 192, T), lambda b, i: (b, 0, 0)),
                  pl.BlockSpec((1, ncp, HEAD_DIM), lambda b, i: (b, 0, 0)),
                  pl.BlockSpec((1, HEAD_DIM, ncp), lambda b, i: (b, 0, 0)),
                  pl.BlockSpec((nslc, ncp), lambda b, i: (0, 0))],
        out_specs=pl.BlockSpec((1, Q_BLOCK, 256), lambda b, i: (b, i, 0)),
        out_shape=jax.ShapeDtypeStruct((B, T, 256), BF16),
        scratch_shapes=[pltpu.VMEM((nslc, Q_BLOCK), F32)] + _attention_scratch(tk)
        + _attention_scratch(Q_BLOCK),
        compiler_params=pltpu.CompilerParams(
            dimension_semantics=("parallel", "arbitrary"), vmem_limit_bytes=VMEM_LIMIT),
        name="nsa",
    )(qbT, smallT, gates, knat, vT, kcmp, vcmpT, ovT)


def _out_kernel(x_ref, ma_ref, mb_ref, mcd_ref, wo_ref, fw_ref, o_ref, *, final):
    y = (x_ref[...] + _dot(ma_ref[...], wo_ref[0:256, :]) + _dot(mb_ref[...], wo_ref[256:512, :])
         + _dot(mcd_ref[...], wo_ref[512:1024, :]))
    if final:
        ms = jnp.mean(y * y, axis=-1, keepdims=True)
        y = y * lax.rsqrt(ms + EPS) * fw_ref[...]
    o_ref[...] = y


def _out_call(x2, ma, mb, mcd, wo, fw, *, final, tm):
    R, D = x2.shape
    rows = lambda w: pl.BlockSpec((tm, w), lambda r: (r, 0))
    return pl.pallas_call(
        functools.partial(_out_kernel, final=final),
        grid=(R // tm,),
        in_specs=[rows(D), rows(256), rows(256), rows(512),
                  pl.BlockSpec((D, D), lambda r: (0, 0)), pl.BlockSpec((1, D), lambda r: (0, 0))],
        out_specs=rows(D),
        out_shape=jax.ShapeDtypeStruct((R, D), F32),
        compiler_params=pltpu.CompilerParams(
            dimension_semantics=("parallel",), vmem_limit_bytes=VMEM_LIMIT),
        name="outproj",
    )(x2, ma, mb, mcd, wo, fw)


def _rope_angles(pos, rot):
    half = rot // 2
    inv = ROPE_THETA ** (-jnp.arange(half, dtype=F32) / half)
    ang = pos.astype(F32)[:, None] * inv[None, :]
    return jnp.cos(ang), jnp.sin(ang)


def _rope_tables(T):
    pos = jnp.arange(T)
    c16, s16 = _rope_angles(pos, ROT_DIM)
    c8, s8 = _rope_angles(pos, IDX_ROT)
    cat = lambda parts: jnp.concatenate(parts, axis=1)
    z = lambda w, n=T: jnp.zeros((n, w), F32)
    o = lambda w, n=T: jnp.ones((n, w), F32)
    C = cat([c16, c16, o(48)] * 3 + [c8, c8, o(56)])
    Sa = cat([-s16, z(56)] * 3 + [z(64)])
    Sb = cat([z(8), s16, z(48)] * 3 + [z(64)])
    Sc = cat([z(KI_OFF), -s8, z(60)])
    Sd = cat([z(KI_OFF + 4), s8, z(56)])
    tabn = jnp.stack([C, Sa, Sb, Sc, Sd])
    tabt = jnp.stack([c16.T, s16.T, jnp.concatenate([c8.T, c8.T]),
                      jnp.concatenate([-s8.T, s8.T])])
    ncp = T // CMP_STRIDE
    cc, sc_ = _rope_angles(jnp.arange(ncp) * CMP_STRIDE + CMP_LEN - 1, ROT_DIM)
    tabc = jnp.stack([cat([cc, cc, o(112, ncp)]), cat([-sc_, z(120, ncp)]),
                      cat([z(8, ncp), sc_, z(112, ncp)])])
    return tabn, tabt, tabc


def _split_offsets():
    widths = (("a_q", 256), ("a_k", 64), ("a_v", 64), ("a_qi", 256), ("a_ki", 32), ("a_wi", 8),
              ("a_gate", 256), ("b_q", 256), ("b_kc", 64), ("b_vc", 64), ("b_ks", 64),
              ("b_vs", 64), ("b_kw", 64), ("b_vw", 64), ("b_g", 12), ("b_gate", 256),
              ("c_b", 256), ("c_c", 256), ("c_x", 256), ("c_gate", 256), ("d_u", 256),
              ("d_gate", 256))
    offs, o = {}, 0
    for name, w in widths:
        offs[name] = (o, o + w)
        o += w
    return offs


def _relaid_weights(w_in, pool_w, pe_cmp, w_cmp_k, w_cmp_v):
    offs = _split_offsets()
    col = lambda n: w_in[:, :, offs[n][0]:offs[n][1]]
    L, D, _ = w_in.shape
    wn = jnp.concatenate(
        [col("a_k"), col("b_ks"), col("b_kw"), col("a_ki"), jnp.zeros((L, D, 32), F32),
         col("b_kc"), col("b_vc"), col("a_gate"), col("b_gate"), col("c_b"), col("c_c"),
         col("c_x"), col("c_gate"), col("d_u"), col("d_gate")], axis=2).astype(BF16)
    qscale = HEAD_DIM ** -0.5 * LOG2E
    iscale = (IDX_DIM ** -0.5) * (IDX_HEADS ** -0.5)
    wt = jnp.swapaxes(jnp.concatenate(
        [col("a_q") * qscale, col("a_qi"), col("b_q") * qscale, col("a_v"), col("b_vs"),
         col("b_vw"), col("a_wi") * iscale, col("b_g"), jnp.zeros((L, D, 44), F32)],
        axis=2), 1, 2).astype(BF16)
    groups, pch = pool_w.shape[1], pool_w.shape[2]
    poolw = jnp.concatenate(
        [jnp.concatenate([jnp.zeros((L, pch, g * pch), F32), pool_w[:, g],
                          jnp.zeros((L, pch, (groups - 1 - g) * pch), F32)], axis=2)
         for g in range(groups)], axis=1).astype(BF16)
    half = CMP_LEN // 2
    wk4 = w_cmp_k.reshape(L, CMP_LEN, HEAD_DIM, HEAD_DIM)
    wv4 = w_cmp_v.reshape(L, CMP_LEN, HEAD_DIM, HEAD_DIM)
    zk = jnp.zeros((L, half, HEAD_DIM, HEAD_DIM), F32)

    def chunk_w(lo):
        top = jnp.concatenate([wk4[:, lo:lo + half], zk], axis=3)
        bot = jnp.concatenate([zk, wv4[:, lo:lo + half]], axis=3)
        return jnp.concatenate([top, bot], axis=2).reshape(L, half * 128, 128)

    wc = jnp.stack([chunk_w(0), chunk_w(half)], axis=1).astype(BF16)
    pe_rows = lambda lo: jnp.concatenate([pe_cmp[:, lo:lo + half]] * 2, axis=2).reshape(L, 1, half * 128)
    pe2 = jnp.stack([pe_rows(0), pe_rows(half)], axis=1)
    return wn, wt, poolw, wc, pe2


def _overlap_T(T):
    ncp, nslc = T // CMP_STRIDE, T // SLC_BLOCK
    n_cmp = (T - CMP_LEN) // CMP_STRIDE + 1
    cs = np.arange(ncp) * CMP_STRIDE
    ce = cs + CMP_LEN - 1
    ss = np.arange(nslc) * SLC_BLOCK
    ov = (cs[None, :] < ss[:, None] + SLC_BLOCK) & (ce[None, :] >= ss[:, None]) & (np.arange(ncp)[None, :] < n_cmp)
    return jnp.asarray(ov.astype(np.float32))


def kernel(x, norm_w, w_in, w_out, conv_w, pe_cmp, w_cmp_k, w_cmp_v, pool_w, pool_scale, final_norm_w):
    B, T, D = x.shape
    depth = w_in.shape[0]
    tm = min(512, T)
    tk = min(256, T)
    tabn, tabt, tabc = _rope_tables(T)
    ovT = _overlap_T(T)
    fw = final_norm_w.reshape(1, D)
    wn, wt, poolw, wc, pe2 = _relaid_weights(w_in, pool_w, pe_cmp, w_cmp_k, w_cmp_v)
    wo = w_out.astype(BF16)
    for l in range(depth):
        knat, kvc, gates, mcd, qaT, qiT, qbT, vT, smallT = _proj_call(
            x, norm_w[l].reshape(1, D), wn[l], wt[l], tabn, tabt, conv_w[l], poolw[l],
            pool_scale[l].reshape(1, W_GROUP), tm=tm)
        kcmp, vcmpT = _cmp_call(kvc.reshape(B, T // CMP_STRIDE, CMP_STRIDE * 128), pe2[l], wc[l], tabc)
        ma = _dsa_call(qaT, qiT, smallT, gates, knat, vT, tk=tk)
        mb = _nsa_call(qbT, smallT, gates, knat, vT, kcmp, vcmpT, ovT, tk=tk)
        x = _out_call(x.reshape(B * T, D), ma.reshape(B * T, 256), mb.reshape(B * T, 256),
                      mcd.reshape(B * T, 512), wo[l], fw,
                      final=(l == depth - 1), tm=tm).reshape(B, T, D)
    return x
```

```python
import functools

import numpy as np
import jax
import jax.numpy as jnp
from jax import lax
from jax.experimental import pallas as pl
from jax.experimental.pallas import tpu as pltpu

HEAD_DIM = 64
N_HEADS = 4
ROT_DIM = HEAD_DIM // 4
ROPE_THETA = 500000.0
EPS = 1e-6
IDX_HEADS = 8
IDX_DIM = 32
IDX_ROT = IDX_DIM // 4
DSA_TOPK = 256
CMP_LEN = 32
CMP_STRIDE = 16
SLC_BLOCK = 64
SLC_TOPN = 16
WINDOW = 512
CONV_WIDTH = 3
POOL_WINDOWS = (2, 4, 8, 16)
W_GROUP = 256

Q_BLOCK = 256
HALO = 16
NEG = -1e30
M_INIT = -1e29
DEN_ROWS = 16
LOG2E = 1.4426950408889634
VMEM_LIMIT = 48 * 1024 * 1024

F32 = jnp.float32
BF16 = jnp.bfloat16
I32 = jnp.int32
INT_MIN = -2 ** 31

N_K = 0
N_KVC = 256
N_GATE = 384
N_CB, N_CC, N_CX, N_CG, N_DU, N_DG = 896, 1152, 1408, 1664, 1920, 2176
N_NAT = 2432
T_QA, T_QI, T_QB, T_V, T_SMALL, N_TR = 0, 256, 512, 768, 960, 1024
KI_OFF = 192


def _silu(v):
    return v / (1.0 + jnp.exp(-v))


def _dot(a, b):
    return jnp.dot(a, b, preferred_element_type=F32)


def _proj_kernel(x_ref, nw_ref, wn_ref, wt_ref, tabn_ref, tabt_ref, convw_ref, poolw_ref,
                 pscale_ref, knat_o, kvc_o, gates_o, mcd_o, qaT_o, qiT_o, qbT_o, vT_o, smallT_o,
                 cu_ref, du_ref, *, tm):
    t_idx = pl.program_id(1)
    x = x_ref[0]
    ms = jnp.mean(x * x, axis=-1, keepdims=True)
    h = (x * lax.rsqrt(ms + EPS) * nw_ref[...]).astype(BF16)
    zn = _dot(h, wn_ref[...])
    zt = lax.dot_general(wt_ref[...], h, (((1,), (1,)), ((), ())),
                         preferred_element_type=F32)

    zk = zn[:, N_K:N_K + 256]
    kr = (zk * tabn_ref[0]
          + pltpu.roll(zk, 256 - 8, 1) * tabn_ref[1] + pltpu.roll(zk, 8, 1) * tabn_ref[2]
          + pltpu.roll(zk, 256 - 4, 1) * tabn_ref[3] + pltpu.roll(zk, 4, 1) * tabn_ref[4])
    knat_o[0] = kr.astype(BF16)
    kvc_o[0] = zn[:, N_KVC:N_KVC + 128]
    gates_o[0] = _silu(zn[:, N_GATE:N_GATE + 512])

    c_b = zn[:, N_CB:N_CB + 256]
    u = zn[:, N_CC:N_CC + 256] * zn[:, N_CX:N_CX + 256]
    d_u = zn[:, N_DU:N_DU + 256]

    @pl.when(t_idx == 0)
    def _():
        cu_ref[0:HALO, :] = jnp.zeros((HALO, 256), F32)
        du_ref[0:HALO, :] = jnp.zeros((HALO, 256), F32)

    @pl.when(t_idx > 0)
    def _():
        cu_ref[0:HALO, :] = cu_ref[tm:tm + HALO, :]
        du_ref[0:HALO, :] = du_ref[tm:tm + HALO, :]

    cu_ref[HALO:HALO + tm, :] = u
    du_ref[HALO:HALO + tm, :] = d_u
    cw = convw_ref[...]
    y = (cu_ref[HALO - 2:HALO - 2 + tm, :] * cw[0:1, :]
         + cu_ref[HALO - 1:HALO - 1 + tm, :] * cw[1:2, :] + u * cw[2:3, :])
    o_c = c_b * y

    acc = d_u
    sums = {}
    for k in range(1, max(POOL_WINDOWS)):
        acc = acc + du_ref[HALO - k:HALO - k + tm, :]
        if k + 1 in POOL_WINDOWS:
            sums[k + 1] = acc
    lane = lax.broadcasted_iota(I32, (tm, 256), 1)
    row = lax.broadcasted_iota(I32, (tm, 256), 0)
    pch = W_GROUP // len(POOL_WINDOWS)
    ssel = sums[POOL_WINDOWS[-1]]
    wl = jnp.full((tm, 256), POOL_WINDOWS[-1], I32)
    for g in range(len(POOL_WINDOWS) - 2, -1, -1):
        ssel = jnp.where(lane < (g + 1) * pch, sums[POOL_WINDOWS[g]], ssel)
        wl = jnp.where(lane < (g + 1) * pch, POOL_WINDOWS[g], wl)
    cnt = jnp.minimum(t_idx * tm + row + 1, wl).astype(F32)
    pooled = ssel / cnt - d_u
    o_d = _dot(pooled.astype(BF16), poolw_ref[...]) * pscale_ref[...]
    mcd_o[0] = jnp.concatenate(
        [_silu(zn[:, N_CG:N_CG + 256]) * o_c, _silu(zn[:, N_DG:N_DG + 256]) * o_d],
        axis=1).astype(BF16)

    c16, s16, c8, s8 = tabt_ref[0], tabt_ref[1], tabt_ref[2], tabt_ref[3]

    def rope_heads(base, out_ref):
        for hh in range(N_HEADS):
            b = base + HEAD_DIM * hh
            x1, x2 = zt[b:b + 8, :], zt[b + 8:b + 16, :]
            o = jnp.concatenate([x1 * c16 - x2 * s16, x2 * c16 + x1 * s16], axis=0)
            out_ref[0, HEAD_DIM * hh:HEAD_DIM * hh + 16, :] = o.astype(BF16)
            out_ref[0, HEAD_DIM * hh + 16:HEAD_DIM * (hh + 1), :] = zt[b + 16:b + HEAD_DIM, :].astype(BF16)

    rope_heads(T_QA, qaT_o)
    rope_heads(T_QB, qbT_o)
    for hh in range(IDX_HEADS):
        b = T_QI + IDX_DIM * hh
        x8 = zt[b:b + 8, :]
        o = x8 * c8 + pltpu.roll(x8, 4, 0) * s8
        qiT_o[0, IDX_DIM * hh:IDX_DIM * hh + 16, :] = jnp.concatenate(
            [o, zt[b + 8:b + 16, :]], axis=0).astype(BF16)
        qiT_o[0, IDX_DIM * hh + 16:IDX_DIM * (hh + 1), :] = zt[b + 16:b + IDX_DIM, :].astype(BF16)
    vT_o[0] = zt[T_V:T_V + 192, :].astype(BF16)
    smallT_o[0] = zt[T_SMALL:T_SMALL + 64, :]


def _proj_call(x, nw, wn, wt, tabn, tabt, convw, poolw, pscale, *, tm):
    B, T, D = x.shape
    grid = (B, T // tm)
    const = lambda *shape: pl.BlockSpec(shape, lambda b, t: (0,) * len(shape))
    nat = lambda w: pl.BlockSpec((1, tm, w), lambda b, t: (b, t, 0))
    tr = lambda r: pl.BlockSpec((1, r, tm), lambda b, t: (b, 0, t))
    out_shape = (
        jax.ShapeDtypeStruct((B, T, 256), BF16), jax.ShapeDtypeStruct((B, T, 128), F32),
        jax.ShapeDtypeStruct((B, T, 512), F32), jax.ShapeDtypeStruct((B, T, 512), BF16),
        jax.ShapeDtypeStruct((B, 256, T), BF16), jax.ShapeDtypeStruct((B, 256, T), BF16),
        jax.ShapeDtypeStruct((B, 256, T), BF16), jax.ShapeDtypeStruct((B, 192, T), BF16),
        jax.ShapeDtypeStruct((B, 64, T), F32))
    return pl.pallas_call(
        functools.partial(_proj_kernel, tm=tm),
        grid=grid,
        in_specs=[nat(D), const(1, D), const(D, N_NAT), const(N_TR, D),
                  pl.BlockSpec((5, tm, 256), lambda b, t: (0, t, 0)),
                  pl.BlockSpec((4, 8, tm), lambda b, t: (0, 0, t)),
                  const(CONV_WIDTH, 256), const(256, 256), const(1, 256)],
        out_specs=(nat(256), nat(128), nat(512), nat(512), tr(256), tr(256), tr(256), tr(192), tr(64)),
        out_shape=out_shape,
        scratch_shapes=[pltpu.VMEM((tm + HALO, 256), F32), pltpu.VMEM((tm + HALO, 256), F32)],
        compiler_params=pltpu.CompilerParams(
            dimension_semantics=("parallel", "arbitrary"), vmem_limit_bytes=VMEM_LIMIT),
        name="proj",
    )(x, nw, wn, wt, tabn, tabt, convw, poolw, pscale)


def _cmp_kernel(kvc_ref, pe_ref, wc_ref, tab_ref, kcmp_o, vcmpT_o, *, ncp):
    c = kvc_ref[0]
    first = _dot((c + pe_ref[0]).astype(BF16), wc_ref[0])
    second = _dot((c + pe_ref[1]).astype(BF16), wc_ref[1])
    kv = first + pltpu.roll(second, ncp - 1, 0)
    kr = (kv * tab_ref[0] + pltpu.roll(kv, 128 - 8, 1) * tab_ref[1]
          + pltpu.roll(kv, 8, 1) * tab_ref[2])
    kcmp_o[0] = kr[:, 0:HEAD_DIM].astype(BF16)
    vcmpT_o[0] = kr.T[HEAD_DIM:2 * HEAD_DIM, :].astype(BF16)


def _cmp_call(kvc_chunks, pe2, wc, tabc):
    B, ncp, width = kvc_chunks.shape
    return pl.pallas_call(
        functools.partial(_cmp_kernel, ncp=ncp),
        grid=(B,),
        in_specs=[pl.BlockSpec((1, ncp, width), lambda b: (b, 0, 0)),
                  pl.BlockSpec((2, 1, width), lambda b: (0, 0, 0)),
                  pl.BlockSpec((2, width, 128), lambda b: (0, 0, 0)),
                  pl.BlockSpec((3, ncp, 128), lambda b: (0, 0, 0))],
        out_specs=(pl.BlockSpec((1, ncp, HEAD_DIM), lambda b: (b, 0, 0)),
                   pl.BlockSpec((1, HEAD_DIM, ncp), lambda b: (b, 0, 0))),
        out_shape=(jax.ShapeDtypeStruct((B, ncp, HEAD_DIM), BF16),
                   jax.ShapeDtypeStruct((B, HEAD_DIM, ncp), BF16)),
        compiler_params=pltpu.CompilerParams(
            dimension_semantics=("parallel",), vmem_limit_bytes=VMEM_LIMIT),
        name="compress",
    )(kvc_chunks, pe2, wc, tabc)


def _heads_on_lanes(qT_ref):
    return jnp.concatenate(
        [qT_ref[0, HEAD_DIM * hh:HEAD_DIM * (hh + 1), :] for hh in range(N_HEADS)], axis=1)


def _rep_heads(a):
    return jnp.concatenate([a] * N_HEADS, axis=1)


def _masked_attention(qT, k_load, vT_load, bias_fn, lo, hi, tk, s_ref, p_ref, acc_ref):
    width = N_HEADS * Q_BLOCK
    last = hi - 1

    def tile_start(kt):
        return pl.multiple_of(jnp.minimum(kt, last) * tk, tk)

    def scores(kt):
        return _dot(k_load(tile_start(kt)), qT)

    ones_rows = jnp.ones((DEN_ROWS, tk), BF16)

    def values(kt, slot, alpha):
        vT1 = jnp.concatenate([vT_load(tile_start(kt)), ones_rows], axis=0)
        acc_ref[...] = alpha * acc_ref[...] + _dot(vT1, p_ref[slot])

    def step(kt, slot, carry):
        m, alpha_prev = carry
        values(jnp.maximum(kt - 1, lo), 1 - slot, alpha_prev)
        s = s_ref[slot]
        s_ref[1 - slot] = scores(kt + 1)
        bias = jnp.where(kt <= last, bias_fn(tile_start(kt)), NEG)
        sm = s + _rep_heads(bias)
        m_new = jnp.maximum(m, jnp.max(sm, axis=0, keepdims=True))
        p_ref[slot] = jnp.exp2(sm - m_new).astype(BF16)
        return m_new, jnp.exp2(m - m_new)

    s_ref[0] = scores(lo)
    p_ref[1] = jnp.zeros((tk, width), BF16)
    acc_ref[...] = jnp.zeros((HEAD_DIM + DEN_ROWS, width), F32)

    def body(j, carry):
        kt = lo + 2 * j
        return step(kt + 1, 1, step(kt, 0, carry))

    init = (jnp.full((1, width), M_INIT, F32), jnp.ones((1, width), F32))
    trips = (hi - lo + 1) // 2
    _, alpha = lax.fori_loop(0, trips, body, init)
    values(lo + 2 * trips - 1, 1, alpha)
    return acc_ref[0:HEAD_DIM, :] / jnp.maximum(acc_ref[HEAD_DIM:HEAD_DIM + 1, :], 1e-30)


def _attention_scratch(tk):
    width = N_HEADS * Q_BLOCK
    return [pltpu.VMEM((2, tk, width), F32), pltpu.VMEM((2, tk, width), BF16),
            pltpu.VMEM((HEAD_DIM + DEN_ROWS, width), F32)]


def _to_token_major(oT):
    stacked = jnp.concatenate(
        [oT[:, Q_BLOCK * hh:Q_BLOCK * (hh + 1)] for hh in range(N_HEADS)], axis=0)
    return stacked.T


_SWAP_MASK = {16: 0x0000FFFF, 8: 0x00FF00FF, 4: 0x0F0F0F0F, 2: 0x33333333, 1: 0x55555555}


def _bit_swap(words, lo, j):
    t = (words[lo] ^ (words[lo + j] >> j)) & _SWAP_MASK[j]
    words[lo] = words[lo] ^ t
    words[lo + j] = words[lo + j] ^ (t << j)


def _dsa_kernel(qaT_ref, qiT_ref, smallT_ref, ga_ref, knat_ref, vT_ref, out_ref,
                keys_ref, planes_ref, j_ref, lg_ref, s_ref, p_ref, acc_ref, *, tk, ta, ts, topk, jbits):
    i = pl.program_id(1)

    @pl.when((pl.program_id(0) == 0) & (i == 0))
    def _():
        planes_ref[...] = jnp.zeros(planes_ref.shape, I32)

    q0 = i * Q_BLOCK
    n_kt = (q0 + Q_BLOCK + tk - 1) // tk
    n_st = (q0 + Q_BLOCK + ts - 1) // ts
    t_row = q0 + lax.broadcasted_iota(I32, (1, Q_BLOCK), 1)
    row_iota_s = lax.broadcasted_iota(I32, (ts, Q_BLOCK), 0)

    qi_cat = jnp.concatenate(
        [qiT_ref[0, IDX_DIM * hh:IDX_DIM * (hh + 1), :] for hh in range(IDX_HEADS)], axis=1)
    wi = smallT_ref[0, 0:IDX_HEADS, :]

    def tile_start(kt):
        return pl.multiple_of(jnp.minimum(kt, n_kt - 1) * tk, tk)

    def logits(kt):
        return _dot(knat_ref[0, pl.ds(tile_start(kt), tk), KI_OFF:KI_OFF + IDX_DIM], qi_cat)

    lg_ref[0] = logits(0)

    wi_rows = [jnp.broadcast_to(wi[hh:hh + 1, :], (8, Q_BLOCK)) for hh in range(IDX_HEADS)]
    row8 = lax.broadcasted_iota(I32, (8, Q_BLOCK), 0)

    def slab_key(slot, k0, a):
        sc = jnp.zeros((8, Q_BLOCK), F32)
        for hh in range(IDX_HEADS):
            lg = lg_ref[slot, 8 * a:8 * (a + 1), Q_BLOCK * hh:Q_BLOCK * (hh + 1)]
            sc = sc + jnp.maximum(lg, 0.0) * wi_rows[hh]
        sc = jnp.where(sc == 0.0, 0.0, sc)
        bits = lax.bitcast_convert_type(sc, I32)
        key = bits ^ ((bits >> 31) & 0x7FFFFFFF)
        key = jnp.where(k0 + 8 * a + row8 <= t_row, key, INT_MIN)
        keys_ref[pl.ds(k0 + 8 * a, 8), :] = key
        return key

    def score_step(kt, slot):
        k0 = tile_start(kt)
        tile = k0 // tk
        lg_ref[1 - slot] = logits(kt + 1)
        for q in range(8):
            words = {a: slab_key(slot, k0, a) for a in (q, q + 8, q + 16, q + 24)}
            _bit_swap(words, q, 16)
            _bit_swap(words, q + 8, 16)
            _bit_swap(words, q, 8)
            _bit_swap(words, q + 16, 8)
            for a, w in words.items():
                planes_ref[tile, a] = w
        for g in range(0, 32, 8):
            words = {a: planes_ref[tile, a] for a in range(g, g + 8)}
            for j in (4, 2, 1):
                for lo in range(g, g + 8):
                    if not lo & j:
                        _bit_swap(words, lo, j)
            for a, w in words.items():
                planes_ref[tile, a] = ~w if a == 0 else w

    def score_body(j, carry):
        score_step(2 * j, 0)
        score_step(2 * j + 1, 1)
        return carry

    lax.fori_loop(0, (n_kt + 1) // 2, score_body, 0)

    n_tiles = planes_ref.shape[0]

    def bit_step(b, carry):
        alive, c_above, thr_u = carry
        ones = [alive[a] & planes_ref[a, b] for a in range(n_tiles)]
        pcs = [lax.population_count(o) for o in ones]
        while len(pcs) > 1:
            pcs = [pcs[a] + pcs[a + 1] for a in range(0, len(pcs), 2)]
        cnt1 = jnp.sum(pcs[0], axis=0, keepdims=True)
        take1 = c_above + cnt1 >= topk
        alive = tuple(jnp.where(take1, o, al ^ o) for al, o in zip(alive, ones))
        bit = lax.shift_left(jnp.int32(1), 31 - b)
        return alive, jnp.where(take1, c_above, c_above + cnt1), jnp.where(take1, thr_u | bit, thr_u)

    alive0 = tuple(jnp.where(a < n_kt, jnp.full((8, Q_BLOCK), -1, I32), 0) for a in range(n_tiles))
    zero_row = jnp.zeros((1, Q_BLOCK), I32)
    alive, c_above, thr_u = lax.fori_loop(0, 32, bit_step, (alive0, zero_row, zero_row))
    thr = thr_u ^ INT_MIN
    n_eq = [lax.population_count(al) for al in alive]
    while len(n_eq) > 1:
        n_eq = [n_eq[a] + n_eq[a + 1] for a in range(0, len(n_eq), 2)]
    cnt_ge = c_above + jnp.sum(n_eq[0], axis=0, keepdims=True)

    def pad_body(kt, carry):
        keys_ref[pl.ds(pl.multiple_of(kt * tk, tk), tk), :] = jnp.full((tk, Q_BLOCK), INT_MIN, I32)
        return carry

    def count(pred):
        def body(st, acc):
            k0 = pl.multiple_of(st * ts, ts)
            hit = pred(keys_ref[pl.ds(k0, ts), :], k0)
            return acc + jnp.sum(hit.reshape(ts // 8, 8, Q_BLOCK), axis=0)
        acc = lax.fori_loop(0, n_st, body, jnp.zeros((8, Q_BLOCK), I32))
        return jnp.sum(acc, axis=0, keepdims=True)

    j_ref[...] = jnp.full((1, Q_BLOCK), 2 ** 30, I32)

    @pl.when(jnp.max(cnt_ge) > topk)
    def _():
        lax.fori_loop(n_kt, n_st * (ts // tk), pad_body, 0)
        need = topk - c_above

        def j_body(it, jp):
            cand = jp | lax.shift_left(jnp.int32(1), jbits - 1 - it)
            before = count(lambda blk, k0: jnp.where(
                blk == thr, jnp.where(k0 + row_iota_s < cand, 1, 0), 0))
            return jnp.where(before < need, cand, jp)

        j_ref[...] = lax.fori_loop(0, jbits, j_body, jnp.zeros((1, Q_BLOCK), I32))

    j_eff = jnp.minimum(j_ref[...], t_row)

    row_iota_a = lax.broadcasted_iota(I32, (ta, Q_BLOCK), 0)

    def bias_fn(k0):
        key = keys_ref[pl.ds(k0, ta), :]
        return jnp.where(key > thr, 0.0,
                         jnp.where(key == thr, jnp.where(k0 + row_iota_a <= j_eff, 0.0, NEG), NEG))

    oT = _masked_attention(
        _heads_on_lanes(qaT_ref),
        lambda k0: knat_ref[0, pl.ds(k0, ta), 0:HEAD_DIM],
        lambda k0: vT_ref[0, 0:HEAD_DIM, pl.ds(k0, ta)],
        bias_fn, 0, (q0 + Q_BLOCK + ta - 1) // ta, ta, s_ref, p_ref, acc_ref)
    out_ref[0] = (_to_token_major(oT) * ga_ref[0]).astype(BF16)


def _dsa_call(qaT, qiT, smallT, gates, knat, vT, *, tk, ta):
    B, _, T = qaT.shape
    assert tk == 32 * 8, "a key tile is transposed as 32 vregs of 8 rows"
    qblk = lambda r: pl.BlockSpec((1, r, Q_BLOCK), lambda b, i: (b, 0, i))
    return pl.pallas_call(
        functools.partial(_dsa_kernel, tk=tk, ta=ta, ts=min(2 * tk, T), topk=min(DSA_TOPK, T // 4),
                          jbits=T.bit_length()),
        grid=(B, T // Q_BLOCK),
        in_specs=[qblk(256), qblk(256), qblk(64),
                  pl.BlockSpec((1, Q_BLOCK, 256), lambda b, i: (b, i, 0)),
                  pl.BlockSpec((1, T, 256), lambda b, i: (b, 0, 0)),
                  pl.BlockSpec((1, 192, T), lambda b, i: (b, 0, 0))],
        out_specs=pl.BlockSpec((1, Q_BLOCK, 256), lambda b, i: (b, i, 0)),
        out_shape=jax.ShapeDtypeStruct((B, T, 256), BF16),
        scratch_shapes=[pltpu.VMEM((T, Q_BLOCK), I32), pltpu.VMEM((T // tk, 32, 8, Q_BLOCK), I32),
                        pltpu.VMEM((1, Q_BLOCK), I32),
                        pltpu.VMEM((2, tk, IDX_HEADS * Q_BLOCK), F32)] + _attention_scratch(ta),
        compiler_params=pltpu.CompilerParams(
            dimension_semantics=("parallel", "arbitrary"), vmem_limit_bytes=VMEM_LIMIT),
        name="dsa",
    )(qaT, qiT, smallT, gates, knat, vT)


def _nsa_kernel(qbT_ref, smallT_ref, gb_ref, knat_ref, vT_ref, kcmp_ref, vcmpT_ref, ovT_ref,
                out_ref, sel_ref, s_ref, p_ref, acc_ref, sw_ref, pw_ref, accw_ref, *, ta, seq):
    i = pl.program_id(1)
    q0 = i * Q_BLOCK
    ncp = seq // CMP_STRIDE
    nslc = seq // SLC_BLOCK
    topn = min(SLC_TOPN, nslc)
    t_row = q0 + lax.broadcasted_iota(I32, (1, Q_BLOCK), 1)
    qT = _heads_on_lanes(qbT_ref)

    s_c = _dot(kcmp_ref[0], qT)
    n_iota = lax.broadcasted_iota(I32, (ncp, Q_BLOCK), 0)
    cend = n_iota * CMP_STRIDE + (CMP_LEN - 1)
    sm = s_c + _rep_heads(jnp.where(cend <= t_row, 0.0, NEG))
    p_c = jnp.exp2(sm - jnp.maximum(jnp.max(sm, axis=0, keepdims=True), M_INIT))
    p_c = p_c / jnp.maximum(jnp.sum(p_c, axis=0, keepdims=True), 1e-30)
    o_cmp = _dot(vcmpT_ref[0], p_c.astype(BF16))

    psum = p_c[:, 0:Q_BLOCK]
    for hh in range(1, N_HEADS):
        psum = psum + p_c[:, Q_BLOCK * hh:Q_BLOCK * (hh + 1)]
    imp = jnp.dot(ovT_ref[...], psum, preferred_element_type=F32,
                  precision=lax.Precision.HIGHEST)
    jidx = lax.broadcasted_iota(I32, (nslc, Q_BLOCK), 0)
    blk_t = t_row // SLC_BLOCK
    v = jnp.where(jidx == 0, jnp.inf,
                  jnp.where(jidx == blk_t, jnp.inf, jnp.where(jidx <= blk_t, imp, -jnp.inf)))
    sub = lax.broadcasted_iota(I32, (8, Q_BLOCK), 0)
    vg = [v[8 * g:8 * (g + 1), :] for g in range(nslc // 8)]
    rank = [jnp.zeros((8, Q_BLOCK), I32) for _ in vg]
    for jp in range(nslc):
        r = v[jp:jp + 1, :]
        for g in range(len(vg)):
            if g > jp // 8:
                beats = jnp.where(r >= vg[g], 1, 0)
            elif g < jp // 8:
                beats = jnp.where(r > vg[g], 1, 0)
            else:
                beats = jnp.where(r > vg[g], 1,
                                  jnp.where(r == vg[g], jnp.where(sub > jp % 8, 1, 0), 0))
            rank[g] = rank[g] + beats
    rank = jnp.concatenate(rank, axis=0)
    sel_ref[...] = jnp.where(rank < topn, jnp.where(jidx <= blk_t, 0.0, NEG), NEG)

    row_a = lax.broadcasted_iota(I32, (ta, Q_BLOCK), 0)
    n_ta = (q0 + Q_BLOCK + ta - 1) // ta

    def slc_bias(k0):
        j0 = k0 // SLC_BLOCK
        rows = [jnp.broadcast_to(sel_ref[pl.ds(j0 + jj, 1), :], (SLC_BLOCK, Q_BLOCK))
                for jj in range(ta // SLC_BLOCK)]
        return jnp.where(k0 + row_a <= t_row, jnp.concatenate(rows, axis=0), NEG)

    o_slc = _masked_attention(
        qT,
        lambda k0: knat_ref[0, pl.ds(k0, ta), HEAD_DIM:2 * HEAD_DIM],
        lambda k0: vT_ref[0, HEAD_DIM:2 * HEAD_DIM, pl.ds(k0, ta)],
        slc_bias, 0, n_ta, ta, s_ref, p_ref, acc_ref)

    def win_bias(k0):
        pos = k0 + row_a
        return jnp.where(pos <= t_row, jnp.where(t_row - pos < WINDOW, 0.0, NEG), NEG)

    o_win = _masked_attention(
        qT,
        lambda k0: knat_ref[0, pl.ds(k0, ta), 2 * HEAD_DIM:3 * HEAD_DIM],
        lambda k0: vT_ref[0, 2 * HEAD_DIM:3 * HEAD_DIM, pl.ds(k0, ta)],
        win_bias, jnp.maximum((q0 - WINDOW) // ta, 0), n_ta, ta, sw_ref, pw_ref, accw_ref)

    g = 1.0 / (1.0 + jnp.exp(-smallT_ref[0, IDX_HEADS:IDX_HEADS + 3 * N_HEADS, :]))
    parts = []
    for hh in range(N_HEADS):
        cols = slice(Q_BLOCK * hh, Q_BLOCK * (hh + 1))
        parts.append(g[3 * hh:3 * hh + 1, :] * o_cmp[:, cols]
                     + g[3 * hh + 1:3 * hh + 2, :] * o_slc[:, cols]
                     + g[3 * hh + 2:3 * hh + 3, :] * o_win[:, cols])
    o_tok = jnp.concatenate(parts, axis=0).T
    out_ref[0] = (o_tok * gb_ref[0]).astype(BF16)


def _nsa_call(qbT, smallT, gates, knat, vT, kcmp, vcmpT, ovT, *, ta):
    B, _, T = qbT.shape
    ncp, nslc = T // CMP_STRIDE, T // SLC_BLOCK
    qblk = lambda r: pl.BlockSpec((1, r, Q_BLOCK), lambda b, i: (b, 0, i))
    return pl.pallas_call(
        functools.partial(_nsa_kernel, ta=ta, seq=T),
        grid=(B, T // Q_BLOCK),
        in_specs=[qblk(256), qblk(64),
                  pl.BlockSpec((1, Q_BLOCK, 256), lambda b, i: (b, i, 1)),
                  pl.BlockSpec((1, T, 256), lambda b, i: (b, 0, 0)),
                  pl.BlockSpec((1, 192, T), lambda b, i: (b, 0, 0)),
                  pl.BlockSpec((1, ncp, HEAD_DIM), lambda b, i: (b, 0, 0)),
                  pl.BlockSpec((1, HEAD_DIM, ncp), lambda b, i: (b, 0, 0)),
                  pl.BlockSpec((nslc, ncp), lambda b, i: (0, 0))],
        out_specs=pl.BlockSpec((1, Q_BLOCK, 256), lambda b, i: (b, i, 0)),
        out_shape=jax.ShapeDtypeStruct((B, T, 256), BF16),
        scratch_shapes=[pltpu.VMEM((nslc, Q_BLOCK), F32)] + _attention_scratch(ta)
        + _attention_scratch(ta),
        compiler_params=pltpu.CompilerParams(
            dimension_semantics=("parallel", "arbitrary"), vmem_limit_bytes=VMEM_LIMIT),
        name="nsa",
    )(qbT, smallT, gates, knat, vT, kcmp, vcmpT, ovT)


def _out_kernel(x_ref, ma_ref, mb_ref, mcd_ref, wo_ref, fw_ref, o_ref, *, final):
    y = (x_ref[...] + _dot(ma_ref[...], wo_ref[0:256, :]) + _dot(mb_ref[...], wo_ref[256:512, :])
         + _dot(mcd_ref[...], wo_ref[512:1024, :]))
    if final:
        ms = jnp.mean(y * y, axis=-1, keepdims=True)
        y = y * lax.rsqrt(ms + EPS) * fw_ref[...]
    o_ref[...] = y


def _out_call(x2, ma, mb, mcd, wo, fw, *, final, tm):
    R, D = x2.shape
    rows = lambda w: pl.BlockSpec((tm, w), lambda r: (r, 0))
    return pl.pallas_call(
        functools.partial(_out_kernel, final=final),
        grid=(R // tm,),
        in_specs=[rows(D), rows(256), rows(256), rows(512),
                  pl.BlockSpec((D, D), lambda r: (0, 0)), pl.BlockSpec((1, D), lambda r: (0, 0))],
        out_specs=rows(D),
        out_shape=jax.ShapeDtypeStruct((R, D), F32),
        compiler_params=pltpu.CompilerParams(
            dimension_semantics=("parallel",), vmem_limit_bytes=VMEM_LIMIT),
        name="outproj",
    )(x2, ma, mb, mcd, wo, fw)


def _rope_angles(pos, rot):
    half = rot // 2
    inv = ROPE_THETA ** (-jnp.arange(half, dtype=F32) / half)
    ang = pos.astype(F32)[:, None] * inv[None, :]
    return jnp.cos(ang), jnp.sin(ang)


def _rope_tables(T):
    pos = jnp.arange(T)
    c16, s16 = _rope_angles(pos, ROT_DIM)
    c8, s8 = _rope_angles(pos, IDX_ROT)
    cat = lambda parts: jnp.concatenate(parts, axis=1)
    z = lambda w, n=T: jnp.zeros((n, w), F32)
    o = lambda w, n=T: jnp.ones((n, w), F32)
    C = cat([c16, c16, o(48)] * 3 + [c8, c8, o(56)])
    Sa = cat([-s16, z(56)] * 3 + [z(64)])
    Sb = cat([z(8), s16, z(48)] * 3 + [z(64)])
    Sc = cat([z(KI_OFF), -s8, z(60)])
    Sd = cat([z(KI_OFF + 4), s8, z(56)])
    tabn = jnp.stack([C, Sa, Sb, Sc, Sd])
    tabt = jnp.stack([c16.T, s16.T, jnp.concatenate([c8.T, c8.T]),
                      jnp.concatenate([-s8.T, s8.T])])
    ncp = T // CMP_STRIDE
    cc, sc_ = _rope_angles(jnp.arange(ncp) * CMP_STRIDE + CMP_LEN - 1, ROT_DIM)
    tabc = jnp.stack([cat([cc, cc, o(112, ncp)]), cat([-sc_, z(120, ncp)]),
                      cat([z(8, ncp), sc_, z(112, ncp)])])
    return tabn, tabt, tabc


def _split_offsets():
    widths = (("a_q", 256), ("a_k", 64), ("a_v", 64), ("a_qi", 256), ("a_ki", 32), ("a_wi", 8),
              ("a_gate", 256), ("b_q", 256), ("b_kc", 64), ("b_vc", 64), ("b_ks", 64),
              ("b_vs", 64), ("b_kw", 64), ("b_vw", 64), ("b_g", 12), ("b_gate", 256),
              ("c_b", 256), ("c_c", 256), ("c_x", 256), ("c_gate", 256), ("d_u", 256),
              ("d_gate", 256))
    offs, o = {}, 0
    for name, w in widths:
        offs[name] = (o, o + w)
        o += w
    return offs


def _relaid_weights(w_in, pool_w, pe_cmp, w_cmp_k, w_cmp_v):
    offs = _split_offsets()
    col = lambda n: w_in[:, :, offs[n][0]:offs[n][1]]
    L, D, _ = w_in.shape
    wn = jnp.concatenate(
        [col("a_k"), col("b_ks"), col("b_kw"), col("a_ki"), jnp.zeros((L, D, 32), F32),
         col("b_kc"), col("b_vc"), col("a_gate"), col("b_gate"), col("c_b"), col("c_c"),
         col("c_x"), col("c_gate"), col("d_u"), col("d_gate")], axis=2).astype(BF16)
    qscale = HEAD_DIM ** -0.5 * LOG2E
    iscale = (IDX_DIM ** -0.5) * (IDX_HEADS ** -0.5)
    wt = jnp.swapaxes(jnp.concatenate(
        [col("a_q") * qscale, col("a_qi"), col("b_q") * qscale, col("a_v"), col("b_vs"),
         col("b_vw"), col("a_wi") * iscale, col("b_g"), jnp.zeros((L, D, 44), F32)],
        axis=2), 1, 2).astype(BF16)
    groups, pch = pool_w.shape[1], pool_w.shape[2]
    poolw = jnp.concatenate(
        [jnp.concatenate([jnp.zeros((L, pch, g * pch), F32), pool_w[:, g],
                          jnp.zeros((L, pch, (groups - 1 - g) * pch), F32)], axis=2)
         for g in range(groups)], axis=1).astype(BF16)
    half = CMP_LEN // 2
    wk4 = w_cmp_k.reshape(L, CMP_LEN, HEAD_DIM, HEAD_DIM)
    wv4 = w_cmp_v.reshape(L, CMP_LEN, HEAD_DIM, HEAD_DIM)
    zk = jnp.zeros((L, half, HEAD_DIM, HEAD_DIM), F32)

    def chunk_w(lo):
        top = jnp.concatenate([wk4[:, lo:lo + half], zk], axis=3)
        bot = jnp.concatenate([zk, wv4[:, lo:lo + half]], axis=3)
        return jnp.concatenate([top, bot], axis=2).reshape(L, half * 128, 128)

    wc = jnp.stack([chunk_w(0), chunk_w(half)], axis=1).astype(BF16)
    pe_rows = lambda lo: jnp.concatenate([pe_cmp[:, lo:lo + half]] * 2, axis=2).reshape(L, 1, half * 128)
    pe2 = jnp.stack([pe_rows(0), pe_rows(half)], axis=1)
    return wn, wt, poolw, wc, pe2


def _overlap_T(T):
    ncp, nslc = T // CMP_STRIDE, T // SLC_BLOCK
    n_cmp = (T - CMP_LEN) // CMP_STRIDE + 1
    cs = np.arange(ncp) * CMP_STRIDE
    ce = cs + CMP_LEN - 1
    ss = np.arange(nslc) * SLC_BLOCK
    ov = (cs[None, :] < ss[:, None] + SLC_BLOCK) & (ce[None, :] >= ss[:, None]) & (np.arange(ncp)[None, :] < n_cmp)
    return jnp.asarray(ov.astype(np.float32))


def kernel(x, norm_w, w_in, w_out, conv_w, pe_cmp, w_cmp_k, w_cmp_v, pool_w, pool_scale, final_norm_w):
    B, T, D = x.shape
    depth = w_in.shape[0]
    tm = min(512, T)
    tk = min(256, T)
    ta = min(128, T)
    tabn, tabt, tabc = _rope_tables(T)
    ovT = _overlap_T(T)
    fw = final_norm_w.reshape(1, D)
    wn, wt, poolw, wc, pe2 = _relaid_weights(w_in, pool_w, pe_cmp, w_cmp_k, w_cmp_v)
    wo = w_out.astype(BF16)
    for l in range(depth):
        knat, kvc, gates, mcd, qaT, qiT, qbT, vT, smallT = _proj_call(
            x, norm_w[l].reshape(1, D), wn[l], wt[l], tabn, tabt, conv_w[l], poolw[l],
            pool_scale[l].reshape(1, W_GROUP), tm=tm)
        kcmp, vcmpT = _cmp_call(kvc.reshape(B, T // CMP_STRIDE, CMP_STRIDE * 128), pe2[l], wc[l], tabc)
        ma = _dsa_call(qaT, qiT, smallT, gates, knat, vT, tk=tk, ta=ta)
        mb = _nsa_call(qbT, smallT, gates, knat, vT, kcmp, vcmpT, ovT, ta=ta)
        x = _out_call(x.reshape(B * T, D), ma.reshape(B * T, 256), mb.reshape(B * T, 256),
                      mcd.reshape(B * T, 512), wo[l], fw,
                      final=(l == depth - 1), tm=tm).reshape(B, T, D)
    return x
```

```python
import functools

import numpy as np
import jax
import jax.numpy as jnp
from jax import lax
from jax.experimental import pallas as pl
from jax.experimental.pallas import tpu as pltpu

HEAD_DIM = 64
N_HEADS = 4
ROT_DIM = HEAD_DIM // 4
ROPE_THETA = 500000.0
EPS = 1e-6
IDX_HEADS = 8
IDX_DIM = 32
IDX_ROT = IDX_DIM // 4
DSA_TOPK = 256
CMP_LEN = 32
CMP_STRIDE = 16
SLC_BLOCK = 64
SLC_TOPN = 16
WINDOW = 512
CONV_WIDTH = 3
POOL_WINDOWS = (2, 4, 8, 16)
W_GROUP = 256

Q_BLOCK = 256
HALO = 16
NEG = -1e30
M_INIT = -1e29
DEN_ROWS = 16
LOG2E = 1.4426950408889634
VMEM_LIMIT = 48 * 1024 * 1024

F32 = jnp.float32
BF16 = jnp.bfloat16
I32 = jnp.int32
INT_MIN = -2 ** 31

N_K = 0
N_KVC = 256
N_GATE = 384
N_CB, N_CC, N_CX, N_CG, N_DU, N_DG = 896, 1152, 1408, 1664, 1920, 2176
N_NAT = 2432
T_QA, T_QI, T_QB, T_V, T_SMALL, N_TR = 0, 256, 512, 768, 960, 1024
KI_OFF = 192


def _silu(v):
    return v / (1.0 + jnp.exp(-v))


def _dot(a, b):
    return jnp.dot(a, b, preferred_element_type=F32)


def _proj_kernel(*refs, tm, sub, fused):
    if fused:
        x_ref, ma_ref, mb_ref, mcdin_ref, wo_ref = refs[:5]
        refs = refs[5:]
    else:
        x_ref = refs[0]
        refs = refs[1:]
    nw_ref, wn_ref, wt_ref, tabn_ref, tabt_ref, convw_ref, poolw_ref, pscale_ref = refs[:8]
    refs = refs[8:]
    if fused:
        xo_ref = refs[0]
        refs = refs[1:]
    knat_o, kvc_o, gates_o, mcd_o, qaT_o, qiT_o, qbT_o, vT_o, smallT_o, cu_ref, du_ref = refs
    t_idx = pl.program_id(1)

    @pl.when(t_idx == 0)
    def _():
        cu_ref[0:HALO, :] = jnp.zeros((HALO, 256), F32)
        du_ref[0:HALO, :] = jnp.zeros((HALO, 256), F32)

    @pl.when(t_idx > 0)
    def _():
        cu_ref[0:HALO, :] = cu_ref[tm:tm + HALO, :]
        du_ref[0:HALO, :] = du_ref[tm:tm + HALO, :]

    lane = lax.broadcasted_iota(I32, (sub, 256), 1)
    row = lax.broadcasted_iota(I32, (sub, 256), 0)
    pch = W_GROUP // len(POOL_WINDOWS)
    cw = convw_ref[...]

    for r0 in range(0, tm, sub):
        rows = slice(r0, r0 + sub)
        x = x_ref[0, rows, :]
        if fused:
            x = (x + _dot(ma_ref[0, rows, :], wo_ref[0:256, :]) + _dot(mb_ref[0, rows, :], wo_ref[256:512, :])
                 + _dot(mcdin_ref[0, rows, :], wo_ref[512:1024, :]))
            xo_ref[0, rows, :] = x
        ms = jnp.mean(x * x, axis=-1, keepdims=True)
        h = (x * lax.rsqrt(ms + EPS) * nw_ref[...]).astype(BF16)
        zn = _dot(h, wn_ref[...])
        zt = lax.dot_general(wt_ref[...], h, (((1,), (1,)), ((), ())),
                             preferred_element_type=F32)

        zk = zn[:, N_K:N_K + 256]
        kr = (zk * tabn_ref[0, rows, :]
              + pltpu.roll(zk, 256 - 8, 1) * tabn_ref[1, rows, :] + pltpu.roll(zk, 8, 1) * tabn_ref[2, rows, :]
              + pltpu.roll(zk, 256 - 4, 1) * tabn_ref[3, rows, :] + pltpu.roll(zk, 4, 1) * tabn_ref[4, rows, :])
        knat_o[0, rows, :] = kr.astype(BF16)
        kvc_o[0, rows, :] = zn[:, N_KVC:N_KVC + 128]
        gates_o[0, rows, :] = _silu(zn[:, N_GATE:N_GATE + 512])

        c_b = zn[:, N_CB:N_CB + 256]
        u = zn[:, N_CC:N_CC + 256] * zn[:, N_CX:N_CX + 256]
        d_u = zn[:, N_DU:N_DU + 256]
        base = HALO + r0
        cu_ref[base:base + sub, :] = u
        du_ref[base:base + sub, :] = d_u
        y = (cu_ref[base - 2:base - 2 + sub, :] * cw[0:1, :]
             + cu_ref[base - 1:base - 1 + sub, :] * cw[1:2, :] + u * cw[2:3, :])
        o_c = c_b * y

        acc = d_u
        sums = {}
        for k in range(1, max(POOL_WINDOWS)):
            acc = acc + du_ref[base - k:base - k + sub, :]
            if k + 1 in POOL_WINDOWS:
                sums[k + 1] = acc
        ssel = sums[POOL_WINDOWS[-1]]
        wl = jnp.full((sub, 256), POOL_WINDOWS[-1], I32)
        for g in range(len(POOL_WINDOWS) - 2, -1, -1):
            ssel = jnp.where(lane < (g + 1) * pch, sums[POOL_WINDOWS[g]], ssel)
            wl = jnp.where(lane < (g + 1) * pch, POOL_WINDOWS[g], wl)
        cnt = jnp.minimum(t_idx * tm + r0 + row + 1, wl).astype(F32)
        pooled = ssel / cnt - d_u
        o_d = _dot(pooled.astype(BF16), poolw_ref[...]) * pscale_ref[...]
        mcd_o[0, rows, :] = jnp.concatenate(
            [_silu(zn[:, N_CG:N_CG + 256]) * o_c, _silu(zn[:, N_DG:N_DG + 256]) * o_d],
            axis=1).astype(BF16)

        c16, s16 = tabt_ref[0, :, rows], tabt_ref[1, :, rows]
        c8, s8 = tabt_ref[2, :, rows], tabt_ref[3, :, rows]

        def rope_heads(base_row, out_ref):
            for hh in range(N_HEADS):
                b = base_row + HEAD_DIM * hh
                x1, x2 = zt[b:b + 8, :], zt[b + 8:b + 16, :]
                o = jnp.concatenate([x1 * c16 - x2 * s16, x2 * c16 + x1 * s16], axis=0)
                out_ref[0, HEAD_DIM * hh:HEAD_DIM * hh + 16, rows] = o.astype(BF16)
                out_ref[0, HEAD_DIM * hh + 16:HEAD_DIM * (hh + 1), rows] = zt[b + 16:b + HEAD_DIM, :].astype(BF16)

        rope_heads(T_QA, qaT_o)
        rope_heads(T_QB, qbT_o)
        for hh in range(IDX_HEADS):
            b = T_QI + IDX_DIM * hh
            x8 = zt[b:b + 8, :]
            o = x8 * c8 + pltpu.roll(x8, 4, 0) * s8
            qiT_o[0, IDX_DIM * hh:IDX_DIM * hh + 16, rows] = jnp.concatenate(
                [o, zt[b + 8:b + 16, :]], axis=0).astype(BF16)
            qiT_o[0, IDX_DIM * hh + 16:IDX_DIM * (hh + 1), rows] = zt[b + 16:b + IDX_DIM, :].astype(BF16)
        vT_o[0, :, rows] = zt[T_V:T_V + 192, :].astype(BF16)
        smallT_o[0, :, rows] = zt[T_SMALL:T_SMALL + 64, :]


def _proj_call(x, prev, nw, wn, wt, tabn, tabt, convw, poolw, pscale, *, tm, sub):
    B, T, D = x.shape
    grid = (B, T // tm)
    const = lambda *shape: pl.BlockSpec(shape, lambda b, t: (0,) * len(shape))
    nat = lambda w: pl.BlockSpec((1, tm, w), lambda b, t: (b, t, 0))
    tr = lambda r: pl.BlockSpec((1, r, tm), lambda b, t: (b, 0, t))
    fused = prev is not None
    out_shape = (
        jax.ShapeDtypeStruct((B, T, 256), BF16), jax.ShapeDtypeStruct((B, T, 128), F32),
        jax.ShapeDtypeStruct((B, T, 512), F32), jax.ShapeDtypeStruct((B, T, 512), BF16),
        jax.ShapeDtypeStruct((B, 256, T), BF16), jax.ShapeDtypeStruct((B, 256, T), BF16),
        jax.ShapeDtypeStruct((B, 256, T), BF16), jax.ShapeDtypeStruct((B, 192, T), BF16),
        jax.ShapeDtypeStruct((B, 64, T), F32))
    out_specs = (nat(256), nat(128), nat(512), nat(512), tr(256), tr(256), tr(256), tr(192), tr(64))
    in_specs = [const(1, D), const(D, N_NAT), const(N_TR, D),
                pl.BlockSpec((5, tm, 256), lambda b, t: (0, t, 0)),
                pl.BlockSpec((4, 8, tm), lambda b, t: (0, 0, t)),
                const(CONV_WIDTH, 256), const(256, 256), const(1, 256)]
    args = (nw, wn, wt, tabn, tabt, convw, poolw, pscale)
    if fused:
        in_specs = [nat(D), nat(256), nat(256), nat(512), const(D, D)] + in_specs
        args = (x,) + tuple(prev) + args
        out_shape = (jax.ShapeDtypeStruct((B, T, D), F32),) + out_shape
        out_specs = (nat(D),) + out_specs
    else:
        in_specs = [nat(D)] + in_specs
        args = (x,) + args
    return pl.pallas_call(
        functools.partial(_proj_kernel, tm=tm, sub=sub, fused=fused),
        grid=grid,
        in_specs=in_specs,
        out_specs=out_specs,
        out_shape=out_shape,
        scratch_shapes=[pltpu.VMEM((tm + HALO, 256), F32), pltpu.VMEM((tm + HALO, 256), F32)],
        compiler_params=pltpu.CompilerParams(
            dimension_semantics=("parallel", "arbitrary"), vmem_limit_bytes=VMEM_LIMIT),
        name="proj",
    )(*args)


def _cmp_kernel(kvc_ref, pe_ref, wc_ref, tab_ref, kcmp_o, vcmpT_o, *, ncp):
    c = kvc_ref[0]
    first = _dot((c + pe_ref[0]).astype(BF16), wc_ref[0])
    second = _dot((c + pe_ref[1]).astype(BF16), wc_ref[1])
    kv = first + pltpu.roll(second, ncp - 1, 0)
    kr = (kv * tab_ref[0] + pltpu.roll(kv, 128 - 8, 1) * tab_ref[1]
          + pltpu.roll(kv, 8, 1) * tab_ref[2])
    kcmp_o[0] = kr[:, 0:HEAD_DIM].astype(BF16)
    vcmpT_o[0] = kr.T[HEAD_DIM:2 * HEAD_DIM, :].astype(BF16)


def _cmp_call(kvc_chunks, pe2, wc, tabc):
    B, ncp, width = kvc_chunks.shape
    return pl.pallas_call(
        functools.partial(_cmp_kernel, ncp=ncp),
        grid=(B,),
        in_specs=[pl.BlockSpec((1, ncp, width), lambda b: (b, 0, 0)),
                  pl.BlockSpec((2, 1, width), lambda b: (0, 0, 0)),
                  pl.BlockSpec((2, width, 128), lambda b: (0, 0, 0)),
                  pl.BlockSpec((3, ncp, 128), lambda b: (0, 0, 0))],
        out_specs=(pl.BlockSpec((1, ncp, HEAD_DIM), lambda b: (b, 0, 0)),
                   pl.BlockSpec((1, HEAD_DIM, ncp), lambda b: (b, 0, 0))),
        out_shape=(jax.ShapeDtypeStruct((B, ncp, HEAD_DIM), BF16),
                   jax.ShapeDtypeStruct((B, HEAD_DIM, ncp), BF16)),
        compiler_params=pltpu.CompilerParams(
            dimension_semantics=("parallel",), vmem_limit_bytes=VMEM_LIMIT),
        name="compress",
    )(kvc_chunks, pe2, wc, tabc)


def _heads_on_lanes(qT_ref):
    return jnp.concatenate(
        [qT_ref[0, HEAD_DIM * hh:HEAD_DIM * (hh + 1), :] for hh in range(N_HEADS)], axis=1)


def _rep_heads(a):
    return jnp.concatenate([a] * N_HEADS, axis=1)


def _masked_attention(qT, k_load, vT_load, bias_fn, lo, hi, tk, s_ref, p_ref, acc_ref):
    width = N_HEADS * Q_BLOCK
    last = hi - 1

    def tile_start(kt):
        return pl.multiple_of(jnp.minimum(kt, last) * tk, tk)

    def scores(kt):
        return _dot(k_load(tile_start(kt)), qT)

    ones_rows = jnp.ones((DEN_ROWS, tk), BF16)

    def values(kt, slot, alpha):
        vT1 = jnp.concatenate([vT_load(tile_start(kt)), ones_rows], axis=0)
        acc_ref[...] = alpha * acc_ref[...] + _dot(vT1, p_ref[slot])

    def step(kt, slot, carry):
        m, alpha_prev = carry
        values(jnp.maximum(kt - 1, lo), 1 - slot, alpha_prev)
        s = s_ref[slot]
        s_ref[1 - slot] = scores(kt + 1)
        bias = jnp.where(kt <= last, bias_fn(tile_start(kt)), NEG)
        sm = s + _rep_heads(bias)
        m_new = jnp.maximum(m, jnp.max(sm, axis=0, keepdims=True))
        p_ref[slot] = jnp.exp2(sm - m_new).astype(BF16)
        return m_new, jnp.exp2(m - m_new)

    s_ref[0] = scores(lo)
    p_ref[1] = jnp.zeros((tk, width), BF16)
    acc_ref[...] = jnp.zeros((HEAD_DIM + DEN_ROWS, width), F32)

    def body(j, carry):
        kt = lo + 2 * j
        return step(kt + 1, 1, step(kt, 0, carry))

    init = (jnp.full((1, width), M_INIT, F32), jnp.ones((1, width), F32))
    trips = (hi - lo + 1) // 2
    _, alpha = lax.fori_loop(0, trips, body, init)
    values(lo + 2 * trips - 1, 1, alpha)
    return acc_ref[0:HEAD_DIM, :] / jnp.maximum(acc_ref[HEAD_DIM:HEAD_DIM + 1, :], 1e-30)


def _attention_scratch(tk):
    width = N_HEADS * Q_BLOCK
    return [pltpu.VMEM((2, tk, width), F32), pltpu.VMEM((2, tk, width), BF16),
            pltpu.VMEM((HEAD_DIM + DEN_ROWS, width), F32)]


def _to_token_major(oT):
    stacked = jnp.concatenate(
        [oT[:, Q_BLOCK * hh:Q_BLOCK * (hh + 1)] for hh in range(N_HEADS)], axis=0)
    return stacked.T


_SWAP_MASK = {16: 0x0000FFFF, 8: 0x00FF00FF, 4: 0x0F0F0F0F, 2: 0x33333333, 1: 0x55555555}


def _bit_swap(words, lo, j):
    t = (words[lo] ^ (words[lo + j] >> j)) & _SWAP_MASK[j]
    words[lo] = words[lo] ^ t
    words[lo + j] = words[lo + j] ^ (t << j)


def _dsa_kernel(qaT_ref, qiT_ref, smallT_ref, ga_ref, knat_ref, vT_ref, out_ref,
                keys_ref, planes_ref, j_ref, lg_ref, s_ref, p_ref, acc_ref, *, tk, ta, ts, topk, jbits):
    i = pl.program_id(1)

    @pl.when((pl.program_id(0) == 0) & (i == 0))
    def _():
        planes_ref[...] = jnp.zeros(planes_ref.shape, I32)

    q0 = i * Q_BLOCK
    n_kt = (q0 + Q_BLOCK + tk - 1) // tk
    n_st = (q0 + Q_BLOCK + ts - 1) // ts
    t_row = q0 + lax.broadcasted_iota(I32, (1, Q_BLOCK), 1)
    row_iota_s = lax.broadcasted_iota(I32, (ts, Q_BLOCK), 0)

    qi_cat = jnp.concatenate(
        [qiT_ref[0, IDX_DIM * hh:IDX_DIM * (hh + 1), :] for hh in range(IDX_HEADS)], axis=1)
    wi = smallT_ref[0, 0:IDX_HEADS, :]

    def tile_start(kt):
        return pl.multiple_of(jnp.minimum(kt, n_kt - 1) * tk, tk)

    def logits(kt):
        return _dot(knat_ref[0, pl.ds(tile_start(kt), tk), KI_OFF:KI_OFF + IDX_DIM], qi_cat)

    lg_ref[0] = logits(0)

    wi_rows = [jnp.broadcast_to(wi[hh:hh + 1, :], (8, Q_BLOCK)) for hh in range(IDX_HEADS)]
    row8 = lax.broadcasted_iota(I32, (8, Q_BLOCK), 0)

    def slab_key(slot, k0, a):
        sc = jnp.zeros((8, Q_BLOCK), F32)
        for hh in range(IDX_HEADS):
            lg = lg_ref[slot, 8 * a:8 * (a + 1), Q_BLOCK * hh:Q_BLOCK * (hh + 1)]
            sc = sc + jnp.maximum(lg, 0.0) * wi_rows[hh]
        sc = jnp.where(sc == 0.0, 0.0, sc)
        bits = lax.bitcast_convert_type(sc, I32)
        key = bits ^ ((bits >> 31) & 0x7FFFFFFF)
        key = jnp.where(k0 + 8 * a + row8 <= t_row, key, INT_MIN)
        keys_ref[pl.ds(k0 + 8 * a, 8), :] = key
        return key

    def score_step(kt, slot):
        k0 = tile_start(kt)
        tile = k0 // tk
        lg_ref[1 - slot] = logits(kt + 1)
        for q in range(8):
            words = {a: slab_key(slot, k0, a) for a in (q, q + 8, q + 16, q + 24)}
            _bit_swap(words, q, 16)
            _bit_swap(words, q + 8, 16)
            _bit_swap(words, q, 8)
            _bit_swap(words, q + 16, 8)
            for a, w in words.items():
                planes_ref[tile, a] = w
        for g in range(0, 32, 8):
            words = {a: planes_ref[tile, a] for a in range(g, g + 8)}
            for j in (4, 2, 1):
                for lo in range(g, g + 8):
                    if not lo & j:
                        _bit_swap(words, lo, j)
            for a, w in words.items():
                planes_ref[tile, a] = ~w if a == 0 else w

    def score_body(j, carry):
        score_step(2 * j, 0)
        score_step(2 * j + 1, 1)
        return carry

    lax.fori_loop(0, (n_kt + 1) // 2, score_body, 0)

    n_tiles = planes_ref.shape[0]

    def bit_step(b, carry):
        alive, c_above, thr_u = carry
        ones = [alive[a] & planes_ref[a, b] for a in range(n_tiles)]
        pcs = [lax.population_count(o) for o in ones]
        while len(pcs) > 1:
            pcs = [pcs[a] + pcs[a + 1] for a in range(0, len(pcs), 2)]
        cnt1 = jnp.sum(pcs[0], axis=0, keepdims=True)
        take1 = c_above + cnt1 >= topk
        alive = tuple(jnp.where(take1, o, al ^ o) for al, o in zip(alive, ones))
        bit = lax.shift_left(jnp.int32(1), 31 - b)
        return alive, jnp.where(take1, c_above, c_above + cnt1), jnp.where(take1, thr_u | bit, thr_u)

    alive0 = tuple(jnp.where(a < n_kt, jnp.full((8, Q_BLOCK), -1, I32), 0) for a in range(n_tiles))
    zero_row = jnp.zeros((1, Q_BLOCK), I32)
    alive, c_above, thr_u = lax.fori_loop(0, 32, bit_step, (alive0, zero_row, zero_row))
    thr = thr_u ^ INT_MIN
    n_eq = [lax.population_count(al) for al in alive]
    while len(n_eq) > 1:
        n_eq = [n_eq[a] + n_eq[a + 1] for a in range(0, len(n_eq), 2)]
    cnt_ge = c_above + jnp.sum(n_eq[0], axis=0, keepdims=True)

    def pad_body(kt, carry):
        keys_ref[pl.ds(pl.multiple_of(kt * tk, tk), tk), :] = jnp.full((tk, Q_BLOCK), INT_MIN, I32)
        return carry

    def count(pred):
        def body(st, acc):
            k0 = pl.multiple_of(st * ts, ts)
            hit = pred(keys_ref[pl.ds(k0, ts), :], k0)
            return acc + jnp.sum(hit.reshape(ts // 8, 8, Q_BLOCK), axis=0)
        acc = lax.fori_loop(0, n_st, body, jnp.zeros((8, Q_BLOCK), I32))
        return jnp.sum(acc, axis=0, keepdims=True)

    j_ref[...] = jnp.full((1, Q_BLOCK), 2 ** 30, I32)

    @pl.when(jnp.max(cnt_ge) > topk)
    def _():
        lax.fori_loop(n_kt, n_st * (ts // tk), pad_body, 0)
        need = topk - c_above

        def j_body(it, jp):
            cand = jp | lax.shift_left(jnp.int32(1), jbits - 1 - it)
            before = count(lambda blk, k0: jnp.where(
                blk == thr, jnp.where(k0 + row_iota_s < cand, 1, 0), 0))
            return jnp.where(before < need, cand, jp)

        j_ref[...] = lax.fori_loop(0, jbits, j_body, jnp.zeros((1, Q_BLOCK), I32))

    j_eff = jnp.minimum(j_ref[...], t_row)

    row_iota_a = lax.broadcasted_iota(I32, (ta, Q_BLOCK), 0)

    def bias_fn(k0):
        key = keys_ref[pl.ds(k0, ta), :]
        return jnp.where(key > thr, 0.0,
                         jnp.where(key == thr, jnp.where(k0 + row_iota_a <= j_eff, 0.0, NEG), NEG))

    oT = _masked_attention(
        _heads_on_lanes(qaT_ref),
        lambda k0: knat_ref[0, pl.ds(k0, ta), 0:HEAD_DIM],
        lambda k0: vT_ref[0, 0:HEAD_DIM, pl.ds(k0, ta)],
        bias_fn, 0, (q0 + Q_BLOCK + ta - 1) // ta, ta, s_ref, p_ref, acc_ref)
    out_ref[0] = (_to_token_major(oT) * ga_ref[0]).astype(BF16)


def _dsa_call(qaT, qiT, smallT, gates, knat, vT, *, tk, ta):
    B, _, T = qaT.shape
    assert tk == 32 * 8, "a key tile is transposed as 32 vregs of 8 rows"
    qblk = lambda r: pl.BlockSpec((1, r, Q_BLOCK), lambda b, i: (b, 0, i))
    return pl.pallas_call(
        functools.partial(_dsa_kernel, tk=tk, ta=ta, ts=min(2 * tk, T), topk=min(DSA_TOPK, T // 4),
                          jbits=T.bit_length()),
        grid=(B, T // Q_BLOCK),
        in_specs=[qblk(256), qblk(256), qblk(64),
                  pl.BlockSpec((1, Q_BLOCK, 256), lambda b, i: (b, i, 0)),
                  pl.BlockSpec((1, T, 256), lambda b, i: (b, 0, 0)),
                  pl.BlockSpec((1, 192, T), lambda b, i: (b, 0, 0))],
        out_specs=pl.BlockSpec((1, Q_BLOCK, 256), lambda b, i: (b, i, 0)),
        out_shape=jax.ShapeDtypeStruct((B, T, 256), BF16),
        scratch_shapes=[pltpu.VMEM((T, Q_BLOCK), I32), pltpu.VMEM((T // tk, 32, 8, Q_BLOCK), I32),
                        pltpu.VMEM((1, Q_BLOCK), I32),
                        pltpu.VMEM((2, tk, IDX_HEADS * Q_BLOCK), F32)] + _attention_scratch(ta),
        compiler_params=pltpu.CompilerParams(
            dimension_semantics=("parallel", "arbitrary"), vmem_limit_bytes=VMEM_LIMIT),
        name="dsa",
    )(qaT, qiT, smallT, gates, knat, vT)


def _nsa_kernel(qbT_ref, smallT_ref, gb_ref, knat_ref, vT_ref, kcmp_ref, vcmpT_ref, ovT_ref,
                out_ref, sel_ref, s_ref, p_ref, acc_ref, sw_ref, pw_ref, accw_ref, *, ta, seq):
    i = pl.program_id(1)
    q0 = i * Q_BLOCK
    ncp = seq // CMP_STRIDE
    nslc = seq // SLC_BLOCK
    topn = min(SLC_TOPN, nslc)
    t_row = q0 + lax.broadcasted_iota(I32, (1, Q_BLOCK), 1)
    qT = _heads_on_lanes(qbT_ref)

    s_c = _dot(kcmp_ref[0], qT)
    n_iota = lax.broadcasted_iota(I32, (ncp, Q_BLOCK), 0)
    cend = n_iota * CMP_STRIDE + (CMP_LEN - 1)
    sm = s_c + _rep_heads(jnp.where(cend <= t_row, 0.0, NEG))
    p_c = jnp.exp2(sm - jnp.maximum(jnp.max(sm, axis=0, keepdims=True), M_INIT))
    p_c = p_c / jnp.maximum(jnp.sum(p_c, axis=0, keepdims=True), 1e-30)
    o_cmp = _dot(vcmpT_ref[0], p_c.astype(BF16))

    psum = p_c[:, 0:Q_BLOCK]
    for hh in range(1, N_HEADS):
        psum = psum + p_c[:, Q_BLOCK * hh:Q_BLOCK * (hh + 1)]
    imp = jnp.dot(ovT_ref[...], psum, preferred_element_type=F32,
                  precision=lax.Precision.HIGHEST)
    jidx = lax.broadcasted_iota(I32, (nslc, Q_BLOCK), 0)
    blk_t = t_row // SLC_BLOCK
    v = jnp.where(jidx == 0, jnp.inf,
                  jnp.where(jidx == blk_t, jnp.inf, jnp.where(jidx <= blk_t, imp, -jnp.inf)))
    sub = lax.broadcasted_iota(I32, (8, Q_BLOCK), 0)
    vg = [v[8 * g:8 * (g + 1), :] for g in range(nslc // 8)]
    rank = [jnp.zeros((8, Q_BLOCK), I32) for _ in vg]
    for jp in range(nslc):
        r = v[jp:jp + 1, :]
        for g in range(len(vg)):
            if g > jp // 8:
                beats = jnp.where(r >= vg[g], 1, 0)
            elif g < jp // 8:
                beats = jnp.where(r > vg[g], 1, 0)
            else:
                beats = jnp.where(r > vg[g], 1,
                                  jnp.where(r == vg[g], jnp.where(sub > jp % 8, 1, 0), 0))
            rank[g] = rank[g] + beats
    rank = jnp.concatenate(rank, axis=0)
    sel_ref[...] = jnp.where(rank < topn, jnp.where(jidx <= blk_t, 0.0, NEG), NEG)

    row_a = lax.broadcasted_iota(I32, (ta, Q_BLOCK), 0)
    n_ta = (q0 + Q_BLOCK + ta - 1) // ta

    def slc_bias(k0):
        j0 = k0 // SLC_BLOCK
        rows = [jnp.broadcast_to(sel_ref[pl.ds(j0 + jj, 1), :], (SLC_BLOCK, Q_BLOCK))
                for jj in range(ta // SLC_BLOCK)]
        return jnp.where(k0 + row_a <= t_row, jnp.concatenate(rows, axis=0), NEG)

    o_slc = _masked_attention(
        qT,
        lambda k0: knat_ref[0, pl.ds(k0, ta), HEAD_DIM:2 * HEAD_DIM],
        lambda k0: vT_ref[0, HEAD_DIM:2 * HEAD_DIM, pl.ds(k0, ta)],
        slc_bias, 0, n_ta, ta, s_ref, p_ref, acc_ref)

    def win_bias(k0):
        pos = k0 + row_a
        return jnp.where(pos <= t_row, jnp.where(t_row - pos < WINDOW, 0.0, NEG), NEG)

    o_win = _masked_attention(
        qT,
        lambda k0: knat_ref[0, pl.ds(k0, ta), 2 * HEAD_DIM:3 * HEAD_DIM],
        lambda k0: vT_ref[0, 2 * HEAD_DIM:3 * HEAD_DIM, pl.ds(k0, ta)],
        win_bias, jnp.maximum((q0 - WINDOW) // ta, 0), n_ta, ta, sw_ref, pw_ref, accw_ref)

    g = 1.0 / (1.0 + jnp.exp(-smallT_ref[0, IDX_HEADS:IDX_HEADS + 3 * N_HEADS, :]))
    parts = []
    for hh in range(N_HEADS):
        cols = slice(Q_BLOCK * hh, Q_BLOCK * (hh + 1))
        parts.append(g[3 * hh:3 * hh + 1, :] * o_cmp[:, cols]
                     + g[3 * hh + 1:3 * hh + 2, :] * o_slc[:, cols]
                     + g[3 * hh + 2:3 * hh + 3, :] * o_win[:, cols])
    o_tok = jnp.concatenate(parts, axis=0).T
    out_ref[0] = (o_tok * gb_ref[0]).astype(BF16)


def _nsa_call(qbT, smallT, gates, knat, vT, kcmp, vcmpT, ovT, *, ta):
    B, _, T = qbT.shape
    ncp, nslc = T // CMP_STRIDE, T // SLC_BLOCK
    qblk = lambda r: pl.BlockSpec((1, r, Q_BLOCK), lambda b, i: (b, 0, i))
    return pl.pallas_call(
        functools.partial(_nsa_kernel, ta=ta, seq=T),
        grid=(B, T // Q_BLOCK),
        in_specs=[qblk(256), qblk(64),
                  pl.BlockSpec((1, Q_BLOCK, 256), lambda b, i: (b, i, 1)),
                  pl.BlockSpec((1, T, 256), lambda b, i: (b, 0, 0)),
                  pl.BlockSpec((1, 192, T), lambda b, i: (b, 0, 0)),
                  pl.BlockSpec((1, ncp, HEAD_DIM), lambda b, i: (b, 0, 0)),
                  pl.BlockSpec((1, HEAD_DIM, ncp), lambda b, i: (b, 0, 0)),
                  pl.BlockSpec((nslc, ncp), lambda b, i: (0, 0))],
        out_specs=pl.BlockSpec((1, Q_BLOCK, 256), lambda b, i: (b, i, 0)),
        out_shape=jax.ShapeDtypeStruct((B, T, 256), BF16),
        scratch_shapes=[pltpu.VMEM((nslc, Q_BLOCK), F32)] + _attention_scratch(ta)
        + _attention_scratch(ta),
        compiler_params=pltpu.CompilerParams(
            dimension_semantics=("parallel", "arbitrary"), vmem_limit_bytes=VMEM_LIMIT),
        name="nsa",
    )(qbT, smallT, gates, knat, vT, kcmp, vcmpT, ovT)


def _out_kernel(x_ref, ma_ref, mb_ref, mcd_ref, wo_ref, fw_ref, o_ref, *, final):
    y = (x_ref[...] + _dot(ma_ref[...], wo_ref[0:256, :]) + _dot(mb_ref[...], wo_ref[256:512, :])
         + _dot(mcd_ref[...], wo_ref[512:1024, :]))
    if final:
        ms = jnp.mean(y * y, axis=-1, keepdims=True)
        y = y * lax.rsqrt(ms + EPS) * fw_ref[...]
    o_ref[...] = y


def _out_call(x2, ma, mb, mcd, wo, fw, *, final, tm):
    R, D = x2.shape
    rows = lambda w: pl.BlockSpec((tm, w), lambda r: (r, 0))
    return pl.pallas_call(
        functools.partial(_out_kernel, final=final),
        grid=(R // tm,),
        in_specs=[rows(D), rows(256), rows(256), rows(512),
                  pl.BlockSpec((D, D), lambda r: (0, 0)), pl.BlockSpec((1, D), lambda r: (0, 0))],
        out_specs=rows(D),
        out_shape=jax.ShapeDtypeStruct((R, D), F32),
        compiler_params=pltpu.CompilerParams(
            dimension_semantics=("parallel",), vmem_limit_bytes=VMEM_LIMIT),
        name="outproj",
    )(x2, ma, mb, mcd, wo, fw)


def _rope_angles(pos, rot):
    half = rot // 2
    inv = ROPE_THETA ** (-jnp.arange(half, dtype=F32) / half)
    ang = pos.astype(F32)[:, None] * inv[None, :]
    return jnp.cos(ang), jnp.sin(ang)


def _rope_tables(T):
    pos = jnp.arange(T)
    c16, s16 = _rope_angles(pos, ROT_DIM)
    c8, s8 = _rope_angles(pos, IDX_ROT)
    cat = lambda parts: jnp.concatenate(parts, axis=1)
    z = lambda w, n=T: jnp.zeros((n, w), F32)
    o = lambda w, n=T: jnp.ones((n, w), F32)
    C = cat([c16, c16, o(48)] * 3 + [c8, c8, o(56)])
    Sa = cat([-s16, z(56)] * 3 + [z(64)])
    Sb = cat([z(8), s16, z(48)] * 3 + [z(64)])
    Sc = cat([z(KI_OFF), -s8, z(60)])
    Sd = cat([z(KI_OFF + 4), s8, z(56)])
    tabn = jnp.stack([C, Sa, Sb, Sc, Sd])
    tabt = jnp.stack([c16.T, s16.T, jnp.concatenate([c8.T, c8.T]),
                      jnp.concatenate([-s8.T, s8.T])])
    ncp = T // CMP_STRIDE
    cc, sc_ = _rope_angles(jnp.arange(ncp) * CMP_STRIDE + CMP_LEN - 1, ROT_DIM)
    tabc = jnp.stack([cat([cc, cc, o(112, ncp)]), cat([-sc_, z(120, ncp)]),
                      cat([z(8, ncp), sc_, z(112, ncp)])])
    return tabn, tabt, tabc


def _split_offsets():
    widths = (("a_q", 256), ("a_k", 64), ("a_v", 64), ("a_qi", 256), ("a_ki", 32), ("a_wi", 8),
              ("a_gate", 256), ("b_q", 256), ("b_kc", 64), ("b_vc", 64), ("b_ks", 64),
              ("b_vs", 64), ("b_kw", 64), ("b_vw", 64), ("b_g", 12), ("b_gate", 256),
              ("c_b", 256), ("c_c", 256), ("c_x", 256), ("c_gate", 256), ("d_u", 256),
              ("d_gate", 256))
    offs, o = {}, 0
    for name, w in widths:
        offs[name] = (o, o + w)
        o += w
    return offs


def _relaid_weights(w_in, pool_w, pe_cmp, w_cmp_k, w_cmp_v):
    offs = _split_offsets()
    col = lambda n: w_in[:, :, offs[n][0]:offs[n][1]]
    L, D, _ = w_in.shape
    wn = jnp.concatenate(
        [col("a_k"), col("b_ks"), col("b_kw"), col("a_ki"), jnp.zeros((L, D, 32), F32),
         col("b_kc"), col("b_vc"), col("a_gate"), col("b_gate"), col("c_b"), col("c_c"),
         col("c_x"), col("c_gate"), col("d_u"), col("d_gate")], axis=2).astype(BF16)
    qscale = HEAD_DIM ** -0.5 * LOG2E
    iscale = (IDX_DIM ** -0.5) * (IDX_HEADS ** -0.5)
    wt = jnp.swapaxes(jnp.concatenate(
        [col("a_q") * qscale, col("a_qi"), col("b_q") * qscale, col("a_v"), col("b_vs"),
         col("b_vw"), col("a_wi") * iscale, col("b_g"), jnp.zeros((L, D, 44), F32)],
        axis=2), 1, 2).astype(BF16)
    groups, pch = pool_w.shape[1], pool_w.shape[2]
    poolw = jnp.concatenate(
        [jnp.concatenate([jnp.zeros((L, pch, g * pch), F32), pool_w[:, g],
                          jnp.zeros((L, pch, (groups - 1 - g) * pch), F32)], axis=2)
         for g in range(groups)], axis=1).astype(BF16)
    half = CMP_LEN // 2
    wk4 = w_cmp_k.reshape(L, CMP_LEN, HEAD_DIM, HEAD_DIM)
    wv4 = w_cmp_v.reshape(L, CMP_LEN, HEAD_DIM, HEAD_DIM)
    zk = jnp.zeros((L, half, HEAD_DIM, HEAD_DIM), F32)

    def chunk_w(lo):
        top = jnp.concatenate([wk4[:, lo:lo + half], zk], axis=3)
        bot = jnp.concatenate([zk, wv4[:, lo:lo + half]], axis=3)
        return jnp.concatenate([top, bot], axis=2).reshape(L, half * 128, 128)

    wc = jnp.stack([chunk_w(0), chunk_w(half)], axis=1).astype(BF16)
    pe_rows = lambda lo: jnp.concatenate([pe_cmp[:, lo:lo + half]] * 2, axis=2).reshape(L, 1, half * 128)
    pe2 = jnp.stack([pe_rows(0), pe_rows(half)], axis=1)
    return wn, wt, poolw, wc, pe2


def _overlap_T(T):
    ncp, nslc = T // CMP_STRIDE, T // SLC_BLOCK
    n_cmp = (T - CMP_LEN) // CMP_STRIDE + 1
    cs = np.arange(ncp) * CMP_STRIDE
    ce = cs + CMP_LEN - 1
    ss = np.arange(nslc) * SLC_BLOCK
    ov = (cs[None, :] < ss[:, None] + SLC_BLOCK) & (ce[None, :] >= ss[:, None]) & (np.arange(ncp)[None, :] < n_cmp)
    return jnp.asarray(ov.astype(np.float32))


def kernel(x, norm_w, w_in, w_out, conv_w, pe_cmp, w_cmp_k, w_cmp_v, pool_w, pool_scale, final_norm_w):
    B, T, D = x.shape
    depth = w_in.shape[0]
    tm = min(512, T)
    sub = min(256, tm)
    tk = min(256, T)
    ta = min(128, T)
    tabn, tabt, tabc = _rope_tables(T)
    ovT = _overlap_T(T)
    fw = final_norm_w.reshape(1, D)
    wn, wt, poolw, wc, pe2 = _relaid_weights(w_in, pool_w, pe_cmp, w_cmp_k, w_cmp_v)
    wo = w_out.astype(BF16)
    prev = None
    for l in range(depth):
        outs = _proj_call(x, prev, norm_w[l].reshape(1, D), wn[l], wt[l], tabn, tabt, conv_w[l], poolw[l],
                          pool_scale[l].reshape(1, W_GROUP), tm=tm, sub=sub)
        if prev is not None:
            x, outs = outs[0], outs[1:]
        knat, kvc, gates, mcd, qaT, qiT, qbT, vT, smallT = outs
        kcmp, vcmpT = _cmp_call(kvc.reshape(B, T // CMP_STRIDE, CMP_STRIDE * 128), pe2[l], wc[l], tabc)
        ma = _dsa_call(qaT, qiT, smallT, gates, knat, vT, tk=tk, ta=ta)
        mb = _nsa_call(qbT, smallT, gates, knat, vT, kcmp, vcmpT, ovT, ta=ta)
        prev = (ma, mb, mcd, wo[l])
    x = _out_call(x.reshape(B * T, D), ma.reshape(B * T, 256), mb.reshape(B * T, 256),
                  mcd.reshape(B * T, 512), wo[depth - 1], fw, final=True, tm=tm).reshape(B, T, D)
    return x
```

```python
import functools

import numpy as np
import jax
import jax.numpy as jnp
from jax import lax
from jax.experimental import pallas as pl
from jax.experimental.pallas import tpu as pltpu

HEAD_DIM = 64
N_HEADS = 4
ROT_DIM = HEAD_DIM // 4
ROPE_THETA = 500000.0
EPS = 1e-6
IDX_HEADS = 8
IDX_DIM = 32
IDX_ROT = IDX_DIM // 4
DSA_TOPK = 256
CMP_LEN = 32
CMP_STRIDE = 16
SLC_BLOCK = 64
SLC_TOPN = 16
WINDOW = 512
CONV_WIDTH = 3
POOL_WINDOWS = (2, 4, 8, 16)
W_GROUP = 256

Q_BLOCK = 256
HALO = 16
NEG = -1e30
M_INIT = -1e29
DEN_ROWS = 16
LOG2E = 1.4426950408889634
VMEM_LIMIT = 48 * 1024 * 1024

F32 = jnp.float32
BF16 = jnp.bfloat16
I32 = jnp.int32
INT_MIN = -2 ** 31

N_K = 0
N_KVC = 256
N_GATE = 384
N_CB, N_CC, N_CX, N_CG, N_DU, N_DG = 896, 1152, 1408, 1664, 1920, 2176
N_NAT = 2432
T_QA, T_QI, T_QB, T_V, T_SMALL, N_TR = 0, 256, 512, 768, 960, 1024
KI_OFF = 192


def _silu(v):
    return v / (1.0 + jnp.exp(-v))


def _dot(a, b):
    return jnp.dot(a, b, preferred_element_type=F32)


def _proj_kernel(*refs, tm, sub, fused):
    if fused:
        x_ref, ma_ref, mb_ref, mcdin_ref, wo_ref = refs[:5]
        refs = refs[5:]
    else:
        x_ref = refs[0]
        refs = refs[1:]
    nw_ref, wn_ref, wt_ref, tabn_ref, tabt_ref, convw_ref, poolw_ref, pscale_ref = refs[:8]
    refs = refs[8:]
    if fused:
        xo_ref = refs[0]
        refs = refs[1:]
    knat_o, kvc_o, gates_o, mcd_o, qaT_o, qiT_o, qbT_o, vT_o, smallT_o, cu_ref, du_ref = refs
    t_idx = pl.program_id(1)

    @pl.when(t_idx == 0)
    def _():
        cu_ref[0:HALO, :] = jnp.zeros((HALO, 256), F32)
        du_ref[0:HALO, :] = jnp.zeros((HALO, 256), F32)

    @pl.when(t_idx > 0)
    def _():
        cu_ref[0:HALO, :] = cu_ref[tm:tm + HALO, :]
        du_ref[0:HALO, :] = du_ref[tm:tm + HALO, :]

    lane = lax.broadcasted_iota(I32, (sub, 256), 1)
    row = lax.broadcasted_iota(I32, (sub, 256), 0)
    pch = W_GROUP // len(POOL_WINDOWS)
    cw = convw_ref[...]

    for r0 in range(0, tm, sub):
        rows = slice(r0, r0 + sub)
        x = x_ref[0, rows, :]
        if fused:
            x = (x + _dot(ma_ref[0, rows, :], wo_ref[0:256, :]) + _dot(mb_ref[0, rows, :], wo_ref[256:512, :])
                 + _dot(mcdin_ref[0, rows, :], wo_ref[512:1024, :]))
            xo_ref[0, rows, :] = x
        ms = jnp.mean(x * x, axis=-1, keepdims=True)
        h = (x * lax.rsqrt(ms + EPS) * nw_ref[...]).astype(BF16)
        zn = _dot(h, wn_ref[...])
        zt = lax.dot_general(wt_ref[...], h, (((1,), (1,)), ((), ())),
                             preferred_element_type=F32)

        zk = zn[:, N_K:N_K + 256]
        kr = (zk * tabn_ref[0, rows, :]
              + pltpu.roll(zk, 256 - 8, 1) * tabn_ref[1, rows, :] + pltpu.roll(zk, 8, 1) * tabn_ref[2, rows, :]
              + pltpu.roll(zk, 256 - 4, 1) * tabn_ref[3, rows, :] + pltpu.roll(zk, 4, 1) * tabn_ref[4, rows, :])
        knat_o[0, rows, :] = kr.astype(BF16)
        kvc_o[0, rows, :] = zn[:, N_KVC:N_KVC + 128]
        gates_o[0, rows, :] = _silu(zn[:, N_GATE:N_GATE + 512])

        c_b = zn[:, N_CB:N_CB + 256]
        u = zn[:, N_CC:N_CC + 256] * zn[:, N_CX:N_CX + 256]
        d_u = zn[:, N_DU:N_DU + 256]
        base = HALO + r0
        cu_ref[base:base + sub, :] = u
        du_ref[base:base + sub, :] = d_u
        y = (cu_ref[base - 2:base - 2 + sub, :] * cw[0:1, :]
             + cu_ref[base - 1:base - 1 + sub, :] * cw[1:2, :] + u * cw[2:3, :])
        o_c = c_b * y

        acc = d_u
        sums = {}
        for k in range(1, max(POOL_WINDOWS)):
            acc = acc + du_ref[base - k:base - k + sub, :]
            if k + 1 in POOL_WINDOWS:
                sums[k + 1] = acc
        ssel = sums[POOL_WINDOWS[-1]]
        wl = jnp.full((sub, 256), POOL_WINDOWS[-1], I32)
        for g in range(len(POOL_WINDOWS) - 2, -1, -1):
            ssel = jnp.where(lane < (g + 1) * pch, sums[POOL_WINDOWS[g]], ssel)
            wl = jnp.where(lane < (g + 1) * pch, POOL_WINDOWS[g], wl)
        cnt = jnp.minimum(t_idx * tm + r0 + row + 1, wl).astype(F32)
        pooled = ssel / cnt - d_u
        o_d = _dot(pooled.astype(BF16), poolw_ref[...]) * pscale_ref[...]
        mcd_o[0, rows, :] = jnp.concatenate(
            [_silu(zn[:, N_CG:N_CG + 256]) * o_c, _silu(zn[:, N_DG:N_DG + 256]) * o_d],
            axis=1).astype(BF16)

        c16, s16 = tabt_ref[0, :, rows], tabt_ref[1, :, rows]
        c8, s8 = tabt_ref[2, :, rows], tabt_ref[3, :, rows]

        def rope_heads(base_row, out_ref):
            for hh in range(N_HEADS):
                b = base_row + HEAD_DIM * hh
                x1, x2 = zt[b:b + 8, :], zt[b + 8:b + 16, :]
                o = jnp.concatenate([x1 * c16 - x2 * s16, x2 * c16 + x1 * s16], axis=0)
                out_ref[0, HEAD_DIM * hh:HEAD_DIM * hh + 16, rows] = o.astype(BF16)
                out_ref[0, HEAD_DIM * hh + 16:HEAD_DIM * (hh + 1), rows] = zt[b + 16:b + HEAD_DIM, :].astype(BF16)

        rope_heads(T_QA, qaT_o)
        rope_heads(T_QB, qbT_o)
        for hh in range(IDX_HEADS):
            b = T_QI + IDX_DIM * hh
            x8 = zt[b:b + 8, :]
            o = x8 * c8 + pltpu.roll(x8, 4, 0) * s8
            qiT_o[0, IDX_DIM * hh:IDX_DIM * hh + 16, rows] = jnp.concatenate(
                [o, zt[b + 8:b + 16, :]], axis=0).astype(BF16)
            qiT_o[0, IDX_DIM * hh + 16:IDX_DIM * (hh + 1), rows] = zt[b + 16:b + IDX_DIM, :].astype(BF16)
        vT_o[0, :, rows] = zt[T_V:T_V + 192, :].astype(BF16)
        smallT_o[0, :, rows] = zt[T_SMALL:T_SMALL + 64, :]


def _proj_call(x, prev, nw, wn, wt, tabn, tabt, convw, poolw, pscale, *, tm, sub):
    B, T, D = x.shape
    grid = (B, T // tm)
    const = lambda *shape: pl.BlockSpec(shape, lambda b, t: (0,) * len(shape))
    nat = lambda w: pl.BlockSpec((1, tm, w), lambda b, t: (b, t, 0))
    tr = lambda r: pl.BlockSpec((1, r, tm), lambda b, t: (b, 0, t))
    fused = prev is not None
    out_shape = (
        jax.ShapeDtypeStruct((B, T, 256), BF16), jax.ShapeDtypeStruct((B, T, 128), F32),
        jax.ShapeDtypeStruct((B, T, 512), F32), jax.ShapeDtypeStruct((B, T, 512), BF16),
        jax.ShapeDtypeStruct((B, 256, T), BF16), jax.ShapeDtypeStruct((B, 256, T), BF16),
        jax.ShapeDtypeStruct((B, 256, T), BF16), jax.ShapeDtypeStruct((B, 192, T), BF16),
        jax.ShapeDtypeStruct((B, 64, T), F32))
    out_specs = (nat(256), nat(128), nat(512), nat(512), tr(256), tr(256), tr(256), tr(192), tr(64))
    in_specs = [const(1, D), const(D, N_NAT), const(N_TR, D),
                pl.BlockSpec((5, tm, 256), lambda b, t: (0, t, 0)),
                pl.BlockSpec((4, 8, tm), lambda b, t: (0, 0, t)),
                const(CONV_WIDTH, 256), const(256, 256), const(1, 256)]
    args = (nw, wn, wt, tabn, tabt, convw, poolw, pscale)
    if fused:
        in_specs = [nat(D), nat(256), nat(256), nat(512), const(D, D)] + in_specs
        args = (x,) + tuple(prev) + args
        out_shape = (jax.ShapeDtypeStruct((B, T, D), F32),) + out_shape
        out_specs = (nat(D),) + out_specs
    else:
        in_specs = [nat(D)] + in_specs
        args = (x,) + args
    return pl.pallas_call(
        functools.partial(_proj_kernel, tm=tm, sub=sub, fused=fused),
        grid=grid,
        in_specs=in_specs,
        out_specs=out_specs,
        out_shape=out_shape,
        scratch_shapes=[pltpu.VMEM((tm + HALO, 256), F32), pltpu.VMEM((tm + HALO, 256), F32)],
        compiler_params=pltpu.CompilerParams(
            dimension_semantics=("parallel", "arbitrary"), vmem_limit_bytes=VMEM_LIMIT),
        name="proj",
    )(*args)


def _cmp_kernel(kvc_ref, pe_ref, wc_ref, tab_ref, kcmp_o, vcmpT_o, *, ncp):
    c = kvc_ref[0]
    first = _dot((c + pe_ref[0]).astype(BF16), wc_ref[0])
    second = _dot((c + pe_ref[1]).astype(BF16), wc_ref[1])
    kv = first + pltpu.roll(second, ncp - 1, 0)
    kr = (kv * tab_ref[0] + pltpu.roll(kv, 128 - 8, 1) * tab_ref[1]
          + pltpu.roll(kv, 8, 1) * tab_ref[2])
    kcmp_o[0] = kr[:, 0:HEAD_DIM].astype(BF16)
    vcmpT_o[0] = kr.T[HEAD_DIM:2 * HEAD_DIM, :].astype(BF16)


def _cmp_call(kvc_chunks, pe2, wc, tabc):
    B, ncp, width = kvc_chunks.shape
    return pl.pallas_call(
        functools.partial(_cmp_kernel, ncp=ncp),
        grid=(B,),
        in_specs=[pl.BlockSpec((1, ncp, width), lambda b: (b, 0, 0)),
                  pl.BlockSpec((2, 1, width), lambda b: (0, 0, 0)),
                  pl.BlockSpec((2, width, 128), lambda b: (0, 0, 0)),
                  pl.BlockSpec((3, ncp, 128), lambda b: (0, 0, 0))],
        out_specs=(pl.BlockSpec((1, ncp, HEAD_DIM), lambda b: (b, 0, 0)),
                   pl.BlockSpec((1, HEAD_DIM, ncp), lambda b: (b, 0, 0))),
        out_shape=(jax.ShapeDtypeStruct((B, ncp, HEAD_DIM), BF16),
                   jax.ShapeDtypeStruct((B, HEAD_DIM, ncp), BF16)),
        compiler_params=pltpu.CompilerParams(
            dimension_semantics=("parallel",), vmem_limit_bytes=VMEM_LIMIT),
        name="compress",
    )(kvc_chunks, pe2, wc, tabc)


def _heads_on_lanes(qT_ref):
    return jnp.concatenate(
        [qT_ref[0, HEAD_DIM * hh:HEAD_DIM * (hh + 1), :] for hh in range(N_HEADS)], axis=1)


def _rep_heads(a):
    return jnp.concatenate([a] * N_HEADS, axis=1)


def _tree_sum(xs):
    xs = list(xs)
    while len(xs) > 1:
        xs = [xs[a] + xs[a + 1] for a in range(0, len(xs) - 1, 2)] + ([xs[-1]] if len(xs) % 2 else [])
    return xs[0]


def _masked_attention(qT, k_load, vT_load, bias_fn, lo, hi, tk, s_ref, p_ref, acc_ref):
    width = N_HEADS * Q_BLOCK
    last = hi - 1

    def tile_start(kt):
        return pl.multiple_of(jnp.minimum(kt, last) * tk, tk)

    def scores(kt):
        return _dot(k_load(tile_start(kt)), qT)

    ones_rows = jnp.ones((DEN_ROWS, tk), BF16)

    def values(kt, slot, alpha):
        vT1 = jnp.concatenate([vT_load(tile_start(kt)), ones_rows], axis=0)
        acc_ref[...] = alpha * acc_ref[...] + _dot(vT1, p_ref[slot])

    def step(kt, slot, carry):
        m, alpha_prev = carry
        values(jnp.maximum(kt - 1, lo), 1 - slot, alpha_prev)
        s = s_ref[slot]
        s_ref[1 - slot] = scores(kt + 1)
        bias = jnp.where(kt <= last, bias_fn(tile_start(kt)), NEG)
        sm = s + _rep_heads(bias)
        m_new = jnp.maximum(m, jnp.max(sm, axis=0, keepdims=True))
        p_ref[slot] = jnp.exp2(sm - m_new).astype(BF16)
        return m_new, jnp.exp2(m - m_new)

    s_ref[0] = scores(lo)
    p_ref[1] = jnp.zeros((tk, width), BF16)
    acc_ref[...] = jnp.zeros((HEAD_DIM + DEN_ROWS, width), F32)

    def body(j, carry):
        kt = lo + 2 * j
        return step(kt + 1, 1, step(kt, 0, carry))

    init = (jnp.full((1, width), M_INIT, F32), jnp.ones((1, width), F32))
    trips = (hi - lo + 1) // 2
    _, alpha = lax.fori_loop(0, trips, body, init)
    values(lo + 2 * trips - 1, 1, alpha)
    return acc_ref[0:HEAD_DIM, :] / jnp.maximum(acc_ref[HEAD_DIM:HEAD_DIM + 1, :], 1e-30)


def _attention_scratch(tk):
    width = N_HEADS * Q_BLOCK
    return [pltpu.VMEM((2, tk, width), F32), pltpu.VMEM((2, tk, width), BF16),
            pltpu.VMEM((HEAD_DIM + DEN_ROWS, width), F32)]


def _to_token_major(oT):
    stacked = jnp.concatenate(
        [oT[:, Q_BLOCK * hh:Q_BLOCK * (hh + 1)] for hh in range(N_HEADS)], axis=0)
    return stacked.T


_SWAP_MASK = {16: 0x0000FFFF, 8: 0x00FF00FF, 4: 0x0F0F0F0F, 2: 0x33333333, 1: 0x55555555}


def _bit_swap(words, lo, j):
    t = (words[lo] ^ (words[lo + j] >> j)) & _SWAP_MASK[j]
    words[lo] = words[lo] ^ t
    words[lo + j] = words[lo + j] ^ (t << j)


def _dsa_kernel(qaT_ref, qiT_ref, smallT_ref, ga_ref, knat_ref, vT_ref, out_ref,
                keys_ref, planes_ref, res_ref, j_ref, lg_ref, s_ref, p_ref, acc_ref, *, tk, ta, ts, topk, jbits):
    i = pl.program_id(1)

    @pl.when((pl.program_id(0) == 0) & (i == 0))
    def _():
        planes_ref[...] = jnp.zeros(planes_ref.shape, I32)

    q0 = i * Q_BLOCK
    n_kt = (q0 + Q_BLOCK + tk - 1) // tk
    n_st = (q0 + Q_BLOCK + ts - 1) // ts
    t_row = q0 + lax.broadcasted_iota(I32, (1, Q_BLOCK), 1)
    row_iota_s = lax.broadcasted_iota(I32, (ts, Q_BLOCK), 0)

    qi_cat = jnp.concatenate(
        [qiT_ref[0, IDX_DIM * hh:IDX_DIM * (hh + 1), :] for hh in range(IDX_HEADS)], axis=1)
    wi = smallT_ref[0, 0:IDX_HEADS, :]

    def tile_start(kt):
        return pl.multiple_of(jnp.minimum(kt, n_kt - 1) * tk, tk)

    def logits(kt):
        return _dot(knat_ref[0, pl.ds(tile_start(kt), tk), KI_OFF:KI_OFF + IDX_DIM], qi_cat)

    lg_ref[0] = logits(0)

    wi_rows = [jnp.broadcast_to(wi[hh:hh + 1, :], (8, Q_BLOCK)) for hh in range(IDX_HEADS)]
    row8 = lax.broadcasted_iota(I32, (8, Q_BLOCK), 0)

    def slab_key(slot, k0, a):
        sc = jnp.zeros((8, Q_BLOCK), F32)
        for hh in range(IDX_HEADS):
            lg = lg_ref[slot, 8 * a:8 * (a + 1), Q_BLOCK * hh:Q_BLOCK * (hh + 1)]
            sc = sc + jnp.maximum(lg, 0.0) * wi_rows[hh]
        sc = jnp.where(sc == 0.0, 0.0, sc)
        bits = lax.bitcast_convert_type(sc, I32)
        key = bits ^ ((bits >> 31) & 0x7FFFFFFF)
        key = jnp.where(k0 + 8 * a + row8 <= t_row, key, INT_MIN)
        keys_ref[pl.ds(k0 + 8 * a, 8), :] = key
        return key

    def score_step(kt, slot):
        k0 = tile_start(kt)
        tile = k0 // tk
        lg_ref[1 - slot] = logits(kt + 1)
        for q in range(8):
            words = {a: slab_key(slot, k0, a) for a in (q, q + 8, q + 16, q + 24)}
            _bit_swap(words, q, 16)
            _bit_swap(words, q + 8, 16)
            _bit_swap(words, q, 8)
            _bit_swap(words, q + 16, 8)
            for a, w in words.items():
                planes_ref[tile, a] = w
        for g in range(0, 32, 8):
            words = {a: planes_ref[tile, a] for a in range(g, g + 8)}
            for j in (4, 2, 1):
                for lo in range(g, g + 8):
                    if not lo & j:
                        _bit_swap(words, lo, j)
            for a, w in words.items():
                planes_ref[tile, a] = ~w if a == 0 else w

    def score_body(j, carry):
        score_step(2 * j, 0)
        score_step(2 * j + 1, 1)
        return carry

    lax.fori_loop(0, (n_kt + 1) // 2, score_body, 0)

    def search(nt):
        def bit_step(b, carry):
            alive, c_above, thr_u = carry
            ones = [alive[a] & planes_ref[a, b] for a in range(nt)]
            cnt1 = jnp.sum(_tree_sum([lax.population_count(o) for o in ones]), axis=0, keepdims=True)
            take1 = c_above + cnt1 >= topk
            alive = tuple(jnp.where(take1, o, al ^ o) for al, o in zip(alive, ones))
            bit = lax.shift_left(jnp.int32(1), 31 - b)
            return alive, jnp.where(take1, c_above, c_above + cnt1), jnp.where(take1, thr_u | bit, thr_u)

        alive0 = tuple(jnp.where(a < n_kt, jnp.full((8, Q_BLOCK), -1, I32), 0) for a in range(nt))
        zero_row = jnp.zeros((1, Q_BLOCK), I32)
        alive, c_above, thr_u = lax.fori_loop(0, 32, bit_step, (alive0, zero_row, zero_row))
        n_eq = jnp.sum(_tree_sum([lax.population_count(al) for al in alive]), axis=0, keepdims=True)
        res_ref[0:1, :] = thr_u ^ INT_MIN
        res_ref[1:2, :] = c_above
        res_ref[2:3, :] = c_above + n_eq

    n_tiles = planes_ref.shape[0]
    sizes = tuple(range(4, n_tiles + 1, 4)) if n_tiles % 4 == 0 else (n_tiles,)
    for below, nt in zip((0,) + sizes, sizes):
        pl.when((n_kt > below) & (n_kt <= nt))(functools.partial(search, nt))
    thr, c_above, cnt_ge = res_ref[0:1, :], res_ref[1:2, :], res_ref[2:3, :]

    def pad_body(kt, carry):
        keys_ref[pl.ds(pl.multiple_of(kt * tk, tk), tk), :] = jnp.full((tk, Q_BLOCK), INT_MIN, I32)
        return carry

    def count(pred):
        def body(st, acc):
            k0 = pl.multiple_of(st * ts, ts)
            hit = pred(keys_ref[pl.ds(k0, ts), :], k0)
            return acc + jnp.sum(hit.reshape(ts // 8, 8, Q_BLOCK), axis=0)
        acc = lax.fori_loop(0, n_st, body, jnp.zeros((8, Q_BLOCK), I32))
        return jnp.sum(acc, axis=0, keepdims=True)

    j_ref[...] = jnp.full((1, Q_BLOCK), 2 ** 30, I32)

    @pl.when(jnp.max(cnt_ge) > topk)
    def _():
        lax.fori_loop(n_kt, n_st * (ts // tk), pad_body, 0)
        need = topk - c_above

        def j_body(it, jp):
            cand = jp | lax.shift_left(jnp.int32(1), jbits - 1 - it)
            before = count(lambda blk, k0: jnp.where(
                blk == thr, jnp.where(k0 + row_iota_s < cand, 1, 0), 0))
            return jnp.where(before < need, cand, jp)

        j_ref[...] = lax.fori_loop(0, jbits, j_body, jnp.zeros((1, Q_BLOCK), I32))

    j_eff = jnp.minimum(j_ref[...], t_row)

    row_iota_a = lax.broadcasted_iota(I32, (ta, Q_BLOCK), 0)

    def bias_fn(k0):
        key = keys_ref[pl.ds(k0, ta), :]
        return jnp.where(key > thr, 0.0,
                         jnp.where(key == thr, jnp.where(k0 + row_iota_a <= j_eff, 0.0, NEG), NEG))

    oT = _masked_attention(
        _heads_on_lanes(qaT_ref),
        lambda k0: knat_ref[0, pl.ds(k0, ta), 0:HEAD_DIM],
        lambda k0: vT_ref[0, 0:HEAD_DIM, pl.ds(k0, ta)],
        bias_fn, 0, (q0 + Q_BLOCK + ta - 1) // ta, ta, s_ref, p_ref, acc_ref)
    out_ref[0] = (_to_token_major(oT) * ga_ref[0]).astype(BF16)


def _dsa_call(qaT, qiT, smallT, gates, knat, vT, *, tk, ta):
    B, _, T = qaT.shape
    assert tk == 32 * 8, "a key tile is transposed as 32 vregs of 8 rows"
    qblk = lambda r: pl.BlockSpec((1, r, Q_BLOCK), lambda b, i: (b, 0, i))
    return pl.pallas_call(
        functools.partial(_dsa_kernel, tk=tk, ta=ta, ts=min(2 * tk, T), topk=min(DSA_TOPK, T // 4),
                          jbits=T.bit_length()),
        grid=(B, T // Q_BLOCK),
        in_specs=[qblk(256), qblk(256), qblk(64),
                  pl.BlockSpec((1, Q_BLOCK, 256), lambda b, i: (b, i, 0)),
                  pl.BlockSpec((1, T, 256), lambda b, i: (b, 0, 0)),
                  pl.BlockSpec((1, 192, T), lambda b, i: (b, 0, 0))],
        out_specs=pl.BlockSpec((1, Q_BLOCK, 256), lambda b, i: (b, i, 0)),
        out_shape=jax.ShapeDtypeStruct((B, T, 256), BF16),
        scratch_shapes=[pltpu.VMEM((T, Q_BLOCK), I32), pltpu.VMEM((T // tk, 32, 8, Q_BLOCK), I32),
                        pltpu.VMEM((8, Q_BLOCK), I32), pltpu.VMEM((1, Q_BLOCK), I32),
                        pltpu.VMEM((2, tk, IDX_HEADS * Q_BLOCK), F32)] + _attention_scratch(ta),
        compiler_params=pltpu.CompilerParams(
            dimension_semantics=("parallel", "arbitrary"), vmem_limit_bytes=VMEM_LIMIT),
        name="dsa",
    )(qaT, qiT, smallT, gates, knat, vT)


def _nsa_kernel(qbT_ref, smallT_ref, gb_ref, knat_ref, vT_ref, kcmp_ref, vcmpT_ref, ovT_ref,
                out_ref, sel_ref, s_ref, p_ref, acc_ref, sw_ref, pw_ref, accw_ref, *, ta, seq):
    i = pl.program_id(1)
    q0 = i * Q_BLOCK
    ncp = seq // CMP_STRIDE
    nslc = seq // SLC_BLOCK
    topn = min(SLC_TOPN, nslc)
    t_row = q0 + lax.broadcasted_iota(I32, (1, Q_BLOCK), 1)
    qT = _heads_on_lanes(qbT_ref)

    s_c = _dot(kcmp_ref[0], qT)
    n_iota = lax.broadcasted_iota(I32, (ncp, Q_BLOCK), 0)
    cend = n_iota * CMP_STRIDE + (CMP_LEN - 1)
    sm = s_c + _rep_heads(jnp.where(cend <= t_row, 0.0, NEG))
    p_c = jnp.exp2(sm - jnp.maximum(jnp.max(sm, axis=0, keepdims=True), M_INIT))
    p_c = p_c / jnp.maximum(jnp.sum(p_c, axis=0, keepdims=True), 1e-30)
    o_cmp = _dot(vcmpT_ref[0], p_c.astype(BF16))

    psum = p_c[:, 0:Q_BLOCK]
    for hh in range(1, N_HEADS):
        psum = psum + p_c[:, Q_BLOCK * hh:Q_BLOCK * (hh + 1)]
    imp = jnp.dot(ovT_ref[...], psum, preferred_element_type=F32,
                  precision=lax.Precision.HIGHEST)
    jidx = lax.broadcasted_iota(I32, (nslc, Q_BLOCK), 0)
    blk_t = t_row // SLC_BLOCK
    v = jnp.where(jidx == 0, jnp.inf,
                  jnp.where(jidx == blk_t, jnp.inf, jnp.where(jidx <= blk_t, imp, -jnp.inf)))
    sub = lax.broadcasted_iota(I32, (8, Q_BLOCK), 0)
    vg = [v[8 * g:8 * (g + 1), :] for g in range(nslc // 8)]
    rank = [jnp.zeros((8, Q_BLOCK), I32) for _ in vg]
    for jp in range(nslc):
        r = v[jp:jp + 1, :]
        for g in range(len(vg)):
            if g > jp // 8:
                beats = jnp.where(r >= vg[g], 1, 0)
            elif g < jp // 8:
                beats = jnp.where(r > vg[g], 1, 0)
            else:
                beats = jnp.where(r > vg[g], 1,
                                  jnp.where(r == vg[g], jnp.where(sub > jp % 8, 1, 0), 0))
            rank[g] = rank[g] + beats
    rank = jnp.concatenate(rank, axis=0)
    sel_ref[...] = jnp.where(rank < topn, jnp.where(jidx <= blk_t, 0.0, NEG), NEG)

    row_a = lax.broadcasted_iota(I32, (ta, Q_BLOCK), 0)
    n_ta = (q0 + Q_BLOCK + ta - 1) // ta

    def slc_bias(k0):
        j0 = k0 // SLC_BLOCK
        rows = [jnp.broadcast_to(sel_ref[pl.ds(j0 + jj, 1), :], (SLC_BLOCK, Q_BLOCK))
                for jj in range(ta // SLC_BLOCK)]
        return jnp.where(k0 + row_a <= t_row, jnp.concatenate(rows, axis=0), NEG)

    o_slc = _masked_attention(
        qT,
        lambda k0: knat_ref[0, pl.ds(k0, ta), HEAD_DIM:2 * HEAD_DIM],
        lambda k0: vT_ref[0, HEAD_DIM:2 * HEAD_DIM, pl.ds(k0, ta)],
        slc_bias, 0, n_ta, ta, s_ref, p_ref, acc_ref)

    def win_bias(k0):
        pos = k0 + row_a
        return jnp.where(pos <= t_row, jnp.where(t_row - pos < WINDOW, 0.0, NEG), NEG)

    o_win = _masked_attention(
        qT,
        lambda k0: knat_ref[0, pl.ds(k0, ta), 2 * HEAD_DIM:3 * HEAD_DIM],
        lambda k0: vT_ref[0, 2 * HEAD_DIM:3 * HEAD_DIM, pl.ds(k0, ta)],
        win_bias, jnp.maximum((q0 - WINDOW) // ta, 0), n_ta, ta, sw_ref, pw_ref, accw_ref)

    g = 1.0 / (1.0 + jnp.exp(-smallT_ref[0, IDX_HEADS:IDX_HEADS + 3 * N_HEADS, :]))
    parts = []
    for hh in range(N_HEADS):
        cols = slice(Q_BLOCK * hh, Q_BLOCK * (hh + 1))
        parts.append(g[3 * hh:3 * hh + 1, :] * o_cmp[:, cols]
                     + g[3 * hh + 1:3 * hh + 2, :] * o_slc[:, cols]
                     + g[3 * hh + 2:3 * hh + 3, :] * o_win[:, cols])
    o_tok = jnp.concatenate(parts, axis=0).T
    out_ref[0] = (o_tok * gb_ref[0]).astype(BF16)


def _nsa_call(qbT, smallT, gates, knat, vT, kcmp, vcmpT, ovT, *, ta):
    B, _, T = qbT.shape
    ncp, nslc = T // CMP_STRIDE, T // SLC_BLOCK
    qblk = lambda r: pl.BlockSpec((1, r, Q_BLOCK), lambda b, i: (b, 0, i))
    return pl.pallas_call(
        functools.partial(_nsa_kernel, ta=ta, seq=T),
        grid=(B, T // Q_BLOCK),
        in_specs=[qblk(256), qblk(64),
                  pl.BlockSpec((1, Q_BLOCK, 256), lambda b, i: (b, i, 1)),
                  pl.BlockSpec((1, T, 256), lambda b, i: (b, 0, 0)),
                  pl.BlockSpec((1, 192, T), lambda b, i: (b, 0, 0)),
                  pl.BlockSpec((1, ncp, HEAD_DIM), lambda b, i: (b, 0, 0)),
                  pl.BlockSpec((1, HEAD_DIM, ncp), lambda b, i: (b, 0, 0)),
                  pl.BlockSpec((nslc, ncp), lambda b, i: (0, 0))],
        out_specs=pl.BlockSpec((1, Q_BLOCK, 256), lambda b, i: (b, i, 0)),
        out_shape=jax.ShapeDtypeStruct((B, T, 256), BF16),
        scratch_shapes=[pltpu.VMEM((nslc, Q_BLOCK), F32)] + _attention_scratch(ta)
        + _attention_scratch(ta),
        compiler_params=pltpu.CompilerParams(
            dimension_semantics=("parallel", "arbitrary"), vmem_limit_bytes=VMEM_LIMIT),
        name="nsa",
    )(qbT, smallT, gates, knat, vT, kcmp, vcmpT, ovT)


def _out_kernel(x_ref, ma_ref, mb_ref, mcd_ref, wo_ref, fw_ref, o_ref):
    y = (x_ref[...] + _dot(ma_ref[...], wo_ref[0:256, :]) + _dot(mb_ref[...], wo_ref[256:512, :])
         + _dot(mcd_ref[...], wo_ref[512:1024, :]))
    ms = jnp.mean(y * y, axis=-1, keepdims=True)
    o_ref[...] = y * lax.rsqrt(ms + EPS) * fw_ref[...]


def _out_call(x2, ma, mb, mcd, wo, fw, *, tm):
    R, D = x2.shape
    rows = lambda w: pl.BlockSpec((tm, w), lambda r: (r, 0))
    return pl.pallas_call(
        _out_kernel,
        grid=(R // tm,),
        in_specs=[rows(D), rows(256), rows(256), rows(512),
                  pl.BlockSpec((D, D), lambda r: (0, 0)), pl.BlockSpec((1, D), lambda r: (0, 0))],
        out_specs=rows(D),
        out_shape=jax.ShapeDtypeStruct((R, D), F32),
        compiler_params=pltpu.CompilerParams(
            dimension_semantics=("parallel",), vmem_limit_bytes=VMEM_LIMIT),
        name="outproj",
    )(x2, ma, mb, mcd, wo, fw)


def _rope_angles(pos, rot):
    half = rot // 2
    inv = ROPE_THETA ** (-jnp.arange(half, dtype=F32) / half)
    ang = pos.astype(F32)[:, None] * inv[None, :]
    return jnp.cos(ang), jnp.sin(ang)


def _rope_tables(T):
    pos = jnp.arange(T)
    c16, s16 = _rope_angles(pos, ROT_DIM)
    c8, s8 = _rope_angles(pos, IDX_ROT)
    cat = lambda parts: jnp.concatenate(parts, axis=1)
    z = lambda w, n=T: jnp.zeros((n, w), F32)
    o = lambda w, n=T: jnp.ones((n, w), F32)
    C = cat([c16, c16, o(48)] * 3 + [c8, c8, o(56)])
    Sa = cat([-s16, z(56)] * 3 + [z(64)])
    Sb = cat([z(8), s16, z(48)] * 3 + [z(64)])
    Sc = cat([z(KI_OFF), -s8, z(60)])
    Sd = cat([z(KI_OFF + 4), s8, z(56)])
    tabn = jnp.stack([C, Sa, Sb, Sc, Sd])
    tabt = jnp.stack([c16.T, s16.T, jnp.concatenate([c8.T, c8.T]),
                      jnp.concatenate([-s8.T, s8.T])])
    ncp = T // CMP_STRIDE
    cc, sc_ = _rope_angles(jnp.arange(ncp) * CMP_STRIDE + CMP_LEN - 1, ROT_DIM)
    tabc = jnp.stack([cat([cc, cc, o(112, ncp)]), cat([-sc_, z(120, ncp)]),
                      cat([z(8, ncp), sc_, z(112, ncp)])])
    return tabn, tabt, tabc


def _split_offsets():
    widths = (("a_q", 256), ("a_k", 64), ("a_v", 64), ("a_qi", 256), ("a_ki", 32), ("a_wi", 8),
              ("a_gate", 256), ("b_q", 256), ("b_kc", 64), ("b_vc", 64), ("b_ks", 64),
              ("b_vs", 64), ("b_kw", 64), ("b_vw", 64), ("b_g", 12), ("b_gate", 256),
              ("c_b", 256), ("c_c", 256), ("c_x", 256), ("c_gate", 256), ("d_u", 256),
              ("d_gate", 256))
    offs, o = {}, 0
    for name, w in widths:
        offs[name] = (o, o + w)
        o += w
    return offs


def _relaid_weights(w_in, pool_w, pe_cmp, w_cmp_k, w_cmp_v):
    offs = _split_offsets()
    col = lambda n: w_in[:, :, offs[n][0]:offs[n][1]]
    L, D, _ = w_in.shape
    wn = jnp.concatenate(
        [col("a_k"), col("b_ks"), col("b_kw"), col("a_ki"), jnp.zeros((L, D, 32), F32),
         col("b_kc"), col("b_vc"), col("a_gate"), col("b_gate"), col("c_b"), col("c_c"),
         col("c_x"), col("c_gate"), col("d_u"), col("d_gate")], axis=2).astype(BF16)
    qscale = HEAD_DIM ** -0.5 * LOG2E
    iscale = (IDX_DIM ** -0.5) * (IDX_HEADS ** -0.5)
    wt = jnp.swapaxes(jnp.concatenate(
        [col("a_q") * qscale, col("a_qi"), col("b_q") * qscale, col("a_v"), col("b_vs"),
         col("b_vw"), col("a_wi") * iscale, col("b_g"), jnp.zeros((L, D, 44), F32)],
        axis=2), 1, 2).astype(BF16)
    groups, pch = pool_w.shape[1], pool_w.shape[2]
    poolw = jnp.concatenate(
        [jnp.concatenate([jnp.zeros((L, pch, g * pch), F32), pool_w[:, g],
                          jnp.zeros((L, pch, (groups - 1 - g) * pch), F32)], axis=2)
         for g in range(groups)], axis=1).astype(BF16)
    half = CMP_LEN // 2
    wk4 = w_cmp_k.reshape(L, CMP_LEN, HEAD_DIM, HEAD_DIM)
    wv4 = w_cmp_v.reshape(L, CMP_LEN, HEAD_DIM, HEAD_DIM)
    zk = jnp.zeros((L, half, HEAD_DIM, HEAD_DIM), F32)

    def chunk_w(lo):
        top = jnp.concatenate([wk4[:, lo:lo + half], zk], axis=3)
        bot = jnp.concatenate([zk, wv4[:, lo:lo + half]], axis=3)
        return jnp.concatenate([top, bot], axis=2).reshape(L, half * 128, 128)

    wc = jnp.stack([chunk_w(0), chunk_w(half)], axis=1).astype(BF16)
    pe_rows = lambda lo: jnp.concatenate([pe_cmp[:, lo:lo + half]] * 2, axis=2).reshape(L, 1, half * 128)
    pe2 = jnp.stack([pe_rows(0), pe_rows(half)], axis=1)
    return wn, wt, poolw, wc, pe2


def _overlap_T(T):
    ncp, nslc = T // CMP_STRIDE, T // SLC_BLOCK
    n_cmp = (T - CMP_LEN) // CMP_STRIDE + 1
    cs = np.arange(ncp) * CMP_STRIDE
    ce = cs + CMP_LEN - 1
    ss = np.arange(nslc) * SLC_BLOCK
    ov = (cs[None, :] < ss[:, None] + SLC_BLOCK) & (ce[None, :] >= ss[:, None]) & (np.arange(ncp)[None, :] < n_cmp)
    return jnp.asarray(ov.astype(np.float32))


def kernel(x, norm_w, w_in, w_out, conv_w, pe_cmp, w_cmp_k, w_cmp_v, pool_w, pool_scale, final_norm_w):
    B, T, D = x.shape
    depth = w_in.shape[0]
    tm = min(512, T)
    sub = min(256, tm)
    tk = min(256, T)
    ta = min(128, T)
    tabn, tabt, tabc = _rope_tables(T)
    ovT = _overlap_T(T)
    fw = final_norm_w.reshape(1, D)
    wn, wt, poolw, wc, pe2 = _relaid_weights(w_in, pool_w, pe_cmp, w_cmp_k, w_cmp_v)
    wo = w_out.astype(BF16)
    prev = None
    for l in range(depth):
        outs = _proj_call(x, prev, norm_w[l].reshape(1, D), wn[l], wt[l], tabn, tabt, conv_w[l], poolw[l],
                          pool_scale[l].reshape(1, W_GROUP), tm=tm, sub=sub)
        if prev is not None:
            x, outs = outs[0], outs[1:]
        knat, kvc, gates, mcd, qaT, qiT, qbT, vT, smallT = outs
        kcmp, vcmpT = _cmp_call(kvc.reshape(B, T // CMP_STRIDE, CMP_STRIDE * 128), pe2[l], wc[l], tabc)
        ma = _dsa_call(qaT, qiT, smallT, gates, knat, vT, tk=tk, ta=ta)
        mb = _nsa_call(qbT, smallT, gates, knat, vT, kcmp, vcmpT, ovT, ta=ta)
        prev = (ma, mb, mcd, wo[l])
    x = _out_call(x.reshape(B * T, D), ma.reshape(B * T, 256), mb.reshape(B * T, 256),
                  mcd.reshape(B * T, 512), wo[depth - 1], fw, tm=tm).reshape(B, T, D)
    return x
```

```python
import functools

import numpy as np
import jax
import jax.numpy as jnp
from jax import lax
from jax.experimental import pallas as pl
from jax.experimental.pallas import tpu as pltpu

HEAD_DIM = 64
N_HEADS = 4
ROT_DIM = HEAD_DIM // 4
ROPE_THETA = 500000.0
EPS = 1e-6
IDX_HEADS = 8
IDX_DIM = 32
IDX_ROT = IDX_DIM // 4
DSA_TOPK = 256
CMP_LEN = 32
CMP_STRIDE = 16
SLC_BLOCK = 64
SLC_TOPN = 16
WINDOW = 512
CONV_WIDTH = 3
POOL_WINDOWS = (2, 4, 8, 16)
W_GROUP = 256

Q_BLOCK = 256
HALO = 16
NEG = -1e30
M_INIT = -1e29
DEN_ROWS = 16
LOG2E = 1.4426950408889634
VMEM_LIMIT = 48 * 1024 * 1024

F32 = jnp.float32
BF16 = jnp.bfloat16
I32 = jnp.int32
INT_MIN = -2 ** 31

N_K = 0
N_KVC = 256
N_GATE = 384
N_CB, N_CC, N_CX, N_CG, N_DU, N_DG = 896, 1152, 1408, 1664, 1920, 2176
N_NAT = 2432
T_QA, T_QI, T_QB, T_V, T_SMALL, N_TR = 0, 256, 512, 768, 960, 1024
KI_OFF = 192


def _silu(v):
    return v / (1.0 + jnp.exp(-v))


def _dot(a, b):
    return jnp.dot(a, b, preferred_element_type=F32)


def _proj_kernel(*refs, tm, sub, fused):
    if fused:
        x_ref, ma_ref, mb_ref, mcdin_ref, wo_ref = refs[:5]
        refs = refs[5:]
    else:
        x_ref = refs[0]
        refs = refs[1:]
    nw_ref, wn_ref, wt_ref, tabn_ref, tabt_ref, convw_ref, poolw_ref, pscale_ref = refs[:8]
    refs = refs[8:]
    if fused:
        xo_ref = refs[0]
        refs = refs[1:]
    knat_o, kvc_o, gates_o, mcd_o, qaT_o, qiT_o, qbT_o, vT_o, smallT_o, cu_ref, du_ref = refs
    t_idx = pl.program_id(1)

    @pl.when(t_idx == 0)
    def _():
        cu_ref[0:HALO, :] = jnp.zeros((HALO, 256), F32)
        du_ref[0:HALO, :] = jnp.zeros((HALO, 256), F32)

    @pl.when(t_idx > 0)
    def _():
        cu_ref[0:HALO, :] = cu_ref[tm:tm + HALO, :]
        du_ref[0:HALO, :] = du_ref[tm:tm + HALO, :]

    lane = lax.broadcasted_iota(I32, (sub, 256), 1)
    row = lax.broadcasted_iota(I32, (sub, 256), 0)
    pch = W_GROUP // len(POOL_WINDOWS)
    cw = convw_ref[...]

    for r0 in range(0, tm, sub):
        rows = slice(r0, r0 + sub)
        x = x_ref[0, rows, :]
        if fused:
            x = (x + _dot(ma_ref[0, rows, :], wo_ref[0:256, :]) + _dot(mb_ref[0, rows, :], wo_ref[256:512, :])
                 + _dot(mcdin_ref[0, rows, :], wo_ref[512:1024, :]))
            xo_ref[0, rows, :] = x
        ms = jnp.mean(x * x, axis=-1, keepdims=True)
        h = (x * lax.rsqrt(ms + EPS) * nw_ref[...]).astype(BF16)
        zn = _dot(h, wn_ref[...])
        zt = lax.dot_general(wt_ref[...], h, (((1,), (1,)), ((), ())),
                             preferred_element_type=F32)

        zk = zn[:, N_K:N_K + 256]
        kr = (zk * tabn_ref[0, rows, :]
              + pltpu.roll(zk, 256 - 8, 1) * tabn_ref[1, rows, :] + pltpu.roll(zk, 8, 1) * tabn_ref[2, rows, :]
              + pltpu.roll(zk, 256 - 4, 1) * tabn_ref[3, rows, :] + pltpu.roll(zk, 4, 1) * tabn_ref[4, rows, :])
        knat_o[0, rows, :] = kr.astype(BF16)
        kvc_o[0, rows, :] = zn[:, N_KVC:N_KVC + 128]
        gates_o[0, rows, :] = _silu(zn[:, N_GATE:N_GATE + 512])

        c_b = zn[:, N_CB:N_CB + 256]
        u = zn[:, N_CC:N_CC + 256] * zn[:, N_CX:N_CX + 256]
        d_u = zn[:, N_DU:N_DU + 256]
        base = HALO + r0
        cu_ref[base:base + sub, :] = u
        du_ref[base:base + sub, :] = d_u
        y = (cu_ref[base - 2:base - 2 + sub, :] * cw[0:1, :]
             + cu_ref[base - 1:base - 1 + sub, :] * cw[1:2, :] + u * cw[2:3, :])
        o_c = c_b * y

        acc = d_u
        sums = {}
        for k in range(1, max(POOL_WINDOWS)):
            acc = acc + du_ref[base - k:base - k + sub, :]
            if k + 1 in POOL_WINDOWS:
                sums[k + 1] = acc
        ssel = sums[POOL_WINDOWS[-1]]
        wl = jnp.full((sub, 256), POOL_WINDOWS[-1], I32)
        for g in range(len(POOL_WINDOWS) - 2, -1, -1):
            ssel = jnp.where(lane < (g + 1) * pch, sums[POOL_WINDOWS[g]], ssel)
            wl = jnp.where(lane < (g + 1) * pch, POOL_WINDOWS[g], wl)
        cnt = jnp.minimum(t_idx * tm + r0 + row + 1, wl).astype(F32)
        pooled = ssel / cnt - d_u
        o_d = _dot(pooled.astype(BF16), poolw_ref[...]) * pscale_ref[...]
        mcd_o[0, rows, :] = jnp.concatenate(
            [_silu(zn[:, N_CG:N_CG + 256]) * o_c, _silu(zn[:, N_DG:N_DG + 256]) * o_d],
            axis=1).astype(BF16)

        c16, s16 = tabt_ref[0, :, rows], tabt_ref[1, :, rows]
        c8, s8 = tabt_ref[2, :, rows], tabt_ref[3, :, rows]

        def rope_heads(base_row, out_ref):
            for hh in range(N_HEADS):
                b = base_row + HEAD_DIM * hh
                x1, x2 = zt[b:b + 8, :], zt[b + 8:b + 16, :]
                o = jnp.concatenate([x1 * c16 - x2 * s16, x2 * c16 + x1 * s16], axis=0)
                out_ref[0, HEAD_DIM * hh:HEAD_DIM * hh + 16, rows] = o.astype(BF16)
                out_ref[0, HEAD_DIM * hh + 16:HEAD_DIM * (hh + 1), rows] = zt[b + 16:b + HEAD_DIM, :].astype(BF16)

        rope_heads(T_QA, qaT_o)
        rope_heads(T_QB, qbT_o)
        for hh in range(IDX_HEADS):
            b = T_QI + IDX_DIM * hh
            x8 = zt[b:b + 8, :]
            o = x8 * c8 + pltpu.roll(x8, 4, 0) * s8
            qiT_o[0, IDX_DIM * hh:IDX_DIM * hh + 16, rows] = jnp.concatenate(
                [o, zt[b + 8:b + 16, :]], axis=0).astype(BF16)
            qiT_o[0, IDX_DIM * hh + 16:IDX_DIM * (hh + 1), rows] = zt[b + 16:b + IDX_DIM, :].astype(BF16)
        vT_o[0, :, rows] = zt[T_V:T_V + 192, :].astype(BF16)
        smallT_o[0, :, rows] = zt[T_SMALL:T_SMALL + 64, :]


def _proj_call(x, prev, nw, wn, wt, tabn, tabt, convw, poolw, pscale, *, tm, sub):
    B, T, D = x.shape
    grid = (B, T // tm)
    const = lambda *shape: pl.BlockSpec(shape, lambda b, t: (0,) * len(shape))
    nat = lambda w: pl.BlockSpec((1, tm, w), lambda b, t: (b, t, 0))
    tr = lambda r: pl.BlockSpec((1, r, tm), lambda b, t: (b, 0, t))
    fused = prev is not None
    out_shape = (
        jax.ShapeDtypeStruct((B, T, 256), BF16), jax.ShapeDtypeStruct((B, T, 128), F32),
        jax.ShapeDtypeStruct((B, T, 512), F32), jax.ShapeDtypeStruct((B, T, 512), BF16),
        jax.ShapeDtypeStruct((B, 256, T), BF16), jax.ShapeDtypeStruct((B, 256, T), BF16),
        jax.ShapeDtypeStruct((B, 256, T), BF16), jax.ShapeDtypeStruct((B, 192, T), BF16),
        jax.ShapeDtypeStruct((B, 64, T), F32))
    out_specs = (nat(256), nat(128), nat(512), nat(512), tr(256), tr(256), tr(256), tr(192), tr(64))
    in_specs = [const(1, D), const(D, N_NAT), const(N_TR, D),
                pl.BlockSpec((5, tm, 256), lambda b, t: (0, t, 0)),
                pl.BlockSpec((4, 8, tm), lambda b, t: (0, 0, t)),
                const(CONV_WIDTH, 256), const(256, 256), const(1, 256)]
    args = (nw, wn, wt, tabn, tabt, convw, poolw, pscale)
    if fused:
        in_specs = [nat(D), nat(256), nat(256), nat(512), const(D, D)] + in_specs
        args = (x,) + tuple(prev) + args
        out_shape = (jax.ShapeDtypeStruct((B, T, D), F32),) + out_shape
        out_specs = (nat(D),) + out_specs
    else:
        in_specs = [nat(D)] + in_specs
        args = (x,) + args
    return pl.pallas_call(
        functools.partial(_proj_kernel, tm=tm, sub=sub, fused=fused),
        grid=grid,
        in_specs=in_specs,
        out_specs=out_specs,
        out_shape=out_shape,
        scratch_shapes=[pltpu.VMEM((tm + HALO, 256), F32), pltpu.VMEM((tm + HALO, 256), F32)],
        compiler_params=pltpu.CompilerParams(
            dimension_semantics=("parallel", "arbitrary"), vmem_limit_bytes=VMEM_LIMIT),
        name="proj",
    )(*args)


def _cmp_kernel(kvc_ref, pe_ref, wc_ref, tab_ref, kcmp_o, vcmpT_o, *, ncp):
    c = kvc_ref[0]
    first = _dot((c + pe_ref[0]).astype(BF16), wc_ref[0])
    second = _dot((c + pe_ref[1]).astype(BF16), wc_ref[1])
    kv = first + pltpu.roll(second, ncp - 1, 0)
    kr = (kv * tab_ref[0] + pltpu.roll(kv, 128 - 8, 1) * tab_ref[1]
          + pltpu.roll(kv, 8, 1) * tab_ref[2])
    kcmp_o[0] = kr[:, 0:HEAD_DIM].astype(BF16)
    vcmpT_o[0] = kr.T[HEAD_DIM:2 * HEAD_DIM, :].astype(BF16)


def _cmp_call(kvc_chunks, pe2, wc, tabc):
    B, ncp, width = kvc_chunks.shape
    return pl.pallas_call(
        functools.partial(_cmp_kernel, ncp=ncp),
        grid=(B,),
        in_specs=[pl.BlockSpec((1, ncp, width), lambda b: (b, 0, 0)),
                  pl.BlockSpec((2, 1, width), lambda b: (0, 0, 0)),
                  pl.BlockSpec((2, width, 128), lambda b: (0, 0, 0)),
                  pl.BlockSpec((3, ncp, 128), lambda b: (0, 0, 0))],
        out_specs=(pl.BlockSpec((1, ncp, HEAD_DIM), lambda b: (b, 0, 0)),
                   pl.BlockSpec((1, HEAD_DIM, ncp), lambda b: (b, 0, 0))),
        out_shape=(jax.ShapeDtypeStruct((B, ncp, HEAD_DIM), BF16),
                   jax.ShapeDtypeStruct((B, HEAD_DIM, ncp), BF16)),
        compiler_params=pltpu.CompilerParams(
            dimension_semantics=("parallel",), vmem_limit_bytes=VMEM_LIMIT),
        name="compress",
    )(kvc_chunks, pe2, wc, tabc)


def _heads_on_lanes(qT_ref):
    return jnp.concatenate(
        [qT_ref[0, HEAD_DIM * hh:HEAD_DIM * (hh + 1), :] for hh in range(N_HEADS)], axis=1)


def _rep_heads(a):
    return jnp.concatenate([a] * N_HEADS, axis=1)


def _tree_sum(xs):
    xs = list(xs)
    while len(xs) > 1:
        xs = [xs[a] + xs[a + 1] for a in range(0, len(xs) - 1, 2)] + ([xs[-1]] if len(xs) % 2 else [])
    return xs[0]


def _masked_attention(qT, k_load, vT_load, bias_fn, lo, hi, tk, s_ref, p_ref, acc_ref):
    width = N_HEADS * Q_BLOCK
    last = hi - 1

    def tile_start(kt):
        return pl.multiple_of(jnp.minimum(kt, last) * tk, tk)

    def scores(kt):
        return _dot(k_load(tile_start(kt)), qT)

    ones_rows = jnp.ones((DEN_ROWS, tk), BF16)

    def values(kt, slot, alpha):
        vT1 = jnp.concatenate([vT_load(tile_start(kt)), ones_rows], axis=0)
        acc_ref[...] = alpha * acc_ref[...] + _dot(vT1, p_ref[slot])

    def step(kt, slot, carry):
        m, alpha_prev = carry
        values(jnp.maximum(kt - 1, lo), 1 - slot, alpha_prev)
        s = s_ref[slot]
        s_ref[1 - slot] = scores(kt + 1)
        bias = jnp.where(kt <= last, bias_fn(tile_start(kt)), NEG)
        sm = s + _rep_heads(bias)
        m_new = jnp.maximum(m, jnp.max(sm, axis=0, keepdims=True))
        p_ref[slot] = jnp.exp2(sm - m_new).astype(BF16)
        return m_new, jnp.exp2(m - m_new)

    s_ref[0] = scores(lo)
    p_ref[1] = jnp.zeros((tk, width), BF16)
    acc_ref[...] = jnp.zeros((HEAD_DIM + DEN_ROWS, width), F32)

    def body(j, carry):
        kt = lo + 2 * j
        return step(kt + 1, 1, step(kt, 0, carry))

    init = (jnp.full((1, width), M_INIT, F32), jnp.ones((1, width), F32))
    trips = (hi - lo + 1) // 2
    _, alpha = lax.fori_loop(0, trips, body, init)
    values(lo + 2 * trips - 1, 1, alpha)
    return acc_ref[0:HEAD_DIM, :] / jnp.maximum(acc_ref[HEAD_DIM:HEAD_DIM + 1, :], 1e-30)


def _attention_scratch(tk):
    width = N_HEADS * Q_BLOCK
    return [pltpu.VMEM((2, tk, width), F32), pltpu.VMEM((2, tk, width), BF16),
            pltpu.VMEM((HEAD_DIM + DEN_ROWS, width), F32)]


def _to_token_major(oT):
    stacked = jnp.concatenate(
        [oT[:, Q_BLOCK * hh:Q_BLOCK * (hh + 1)] for hh in range(N_HEADS)], axis=0)
    return stacked.T


_SWAP_MASK = {16: 0x0000FFFF, 8: 0x00FF00FF, 4: 0x0F0F0F0F, 2: 0x33333333, 1: 0x55555555}


def _bit_swap(words, lo, j):
    t = (words[lo] ^ (words[lo + j] >> j)) & _SWAP_MASK[j]
    words[lo] = words[lo] ^ t
    words[lo + j] = words[lo + j] ^ (t << j)


def _dsa_kernel(qaT_ref, qiT_ref, smallT_ref, ga_ref, knat_ref, vT_ref, out_ref,
                keys_ref, planes_ref, res_ref, lg_ref, s_ref, p_ref, acc_ref, *, tk, ta, ts, topk, jbits):
    i = pl.program_id(1)

    @pl.when(i == 0)
    def _():
        planes_ref[...] = jnp.zeros(planes_ref.shape, I32)

    q0 = i * Q_BLOCK
    n_kt = (q0 + Q_BLOCK + tk - 1) // tk
    n_st = (q0 + Q_BLOCK + ts - 1) // ts
    t_row = q0 + lax.broadcasted_iota(I32, (1, Q_BLOCK), 1)
    row_iota_s = lax.broadcasted_iota(I32, (ts, Q_BLOCK), 0)

    qi_cat = jnp.concatenate(
        [qiT_ref[0, IDX_DIM * hh:IDX_DIM * (hh + 1), :] for hh in range(IDX_HEADS)], axis=1)
    wi = smallT_ref[0, 0:IDX_HEADS, :]

    def tile_start(kt):
        return pl.multiple_of(jnp.minimum(kt, n_kt - 1) * tk, tk)

    def logits(kt):
        return _dot(knat_ref[0, pl.ds(tile_start(kt), tk), KI_OFF:KI_OFF + IDX_DIM], qi_cat)

    lg_ref[0] = logits(0)

    wi_rows = [jnp.broadcast_to(wi[hh:hh + 1, :], (8, Q_BLOCK)) for hh in range(IDX_HEADS)]
    row8 = lax.broadcasted_iota(I32, (8, Q_BLOCK), 0)

    def slab_key(slot, k0, a):
        sc = jnp.zeros((8, Q_BLOCK), F32)
        for hh in range(IDX_HEADS):
            lg = lg_ref[slot, 8 * a:8 * (a + 1), Q_BLOCK * hh:Q_BLOCK * (hh + 1)]
            sc = sc + jnp.maximum(lg, 0.0) * wi_rows[hh]
        sc = jnp.where(sc == 0.0, 0.0, sc)
        bits = lax.bitcast_convert_type(sc, I32)
        key = bits ^ ((bits >> 31) & 0x7FFFFFFF)
        key = jnp.where(k0 + 8 * a + row8 <= t_row, key, INT_MIN)
        keys_ref[pl.ds(k0 + 8 * a, 8), :] = key
        return key

    def score_step(kt, slot):
        k0 = tile_start(kt)
        tile = k0 // tk
        lg_ref[1 - slot] = logits(kt + 1)
        for q in range(8):
            words = {a: slab_key(slot, k0, a) for a in (q, q + 8, q + 16, q + 24)}
            _bit_swap(words, q, 16)
            _bit_swap(words, q + 8, 16)
            _bit_swap(words, q, 8)
            _bit_swap(words, q + 16, 8)
            for a, w in words.items():
                planes_ref[tile, a] = w
        for g in range(0, 32, 8):
            words = {a: planes_ref[tile, a] for a in range(g, g + 8)}
            for j in (4, 2, 1):
                for lo in range(g, g + 8):
                    if not lo & j:
                        _bit_swap(words, lo, j)
            for a, w in words.items():
                planes_ref[tile, a] = ~w if a == 0 else w

    def score_body(j, carry):
        score_step(2 * j, 0)
        score_step(2 * j + 1, 1)
        return carry

    lax.fori_loop(0, (n_kt + 1) // 2, score_body, 0)

    def search(nt):
        def bit_step(b, carry):
            alive, c_above, thr_u = carry
            ones = [alive[a] & planes_ref[a, b] for a in range(nt)]
            cnt1 = jnp.sum(_tree_sum([lax.population_count(o) for o in ones]), axis=0, keepdims=True)
            take1 = c_above + cnt1 >= topk
            alive = tuple(jnp.where(take1, o, al ^ o) for al, o in zip(alive, ones))
            bit = lax.shift_left(jnp.int32(1), 31 - b)
            return alive, jnp.where(take1, c_above, c_above + cnt1), jnp.where(take1, thr_u | bit, thr_u)

        alive0 = tuple(jnp.where(a < n_kt, jnp.full((8, Q_BLOCK), -1, I32), 0) for a in range(nt))
        zero_row = jnp.zeros((1, Q_BLOCK), I32)
        alive, c_above, thr_u = lax.fori_loop(0, 32, bit_step, (alive0, zero_row, zero_row))
        n_eq = jnp.sum(_tree_sum([lax.population_count(al) for al in alive]), axis=0, keepdims=True)
        res_ref[0:1, :] = thr_u ^ INT_MIN
        res_ref[1:2, :] = c_above
        res_ref[2:3, :] = c_above + n_eq

    n_tiles = planes_ref.shape[0]
    sizes = tuple(range(4, n_tiles + 1, 4)) if n_tiles % 4 == 0 else (n_tiles,)
    for below, nt in zip((0,) + sizes, sizes):
        pl.when((n_kt > below) & (n_kt <= nt))(functools.partial(search, nt))
    thr, c_above, cnt_ge = res_ref[0:1, :], res_ref[1:2, :], res_ref[2:3, :]

    def pad_body(kt, carry):
        keys_ref[pl.ds(pl.multiple_of(kt * tk, tk), tk), :] = jnp.full((tk, Q_BLOCK), INT_MIN, I32)
        return carry

    def count(pred):
        def body(st, acc):
            k0 = pl.multiple_of(st * ts, ts)
            hit = pred(keys_ref[pl.ds(k0, ts), :], k0)
            return acc + jnp.sum(hit.reshape(ts // 8, 8, Q_BLOCK), axis=0)
        acc = lax.fori_loop(0, n_st, body, jnp.zeros((8, Q_BLOCK), I32))
        return jnp.sum(acc, axis=0, keepdims=True)

    def attend(bias_fn):
        oT = _masked_attention(
            _heads_on_lanes(qaT_ref),
            lambda k0: knat_ref[0, pl.ds(k0, ta), 0:HEAD_DIM],
            lambda k0: vT_ref[0, 0:HEAD_DIM, pl.ds(k0, ta)],
            bias_fn, 0, (q0 + Q_BLOCK + ta - 1) // ta, ta, s_ref, p_ref, acc_ref)
        out_ref[0] = (_to_token_major(oT) * ga_ref[0]).astype(BF16)

    ties = (jnp.max(cnt_ge) > topk) | (i == 0)

    @pl.when(jnp.logical_not(ties))
    def _():
        attend(lambda k0: jnp.where(keys_ref[pl.ds(k0, ta), :] >= thr, 0.0, NEG))

    @pl.when(ties)
    def _():
        lax.fori_loop(n_kt, n_st * (ts // tk), pad_body, 0)
        need = topk - c_above

        def j_body(it, jp):
            cand = jp | lax.shift_left(jnp.int32(1), jbits - 1 - it)
            before = count(lambda blk, k0: jnp.where(
                blk == thr, jnp.where(k0 + row_iota_s < cand, 1, 0), 0))
            return jnp.where(before < need, cand, jp)

        j_eff = jnp.minimum(lax.fori_loop(0, jbits, j_body, jnp.zeros((1, Q_BLOCK), I32)), t_row)
        row_iota_a = lax.broadcasted_iota(I32, (ta, Q_BLOCK), 0)

        def bias_fn(k0):
            key = keys_ref[pl.ds(k0, ta), :]
            return jnp.where(key > thr, 0.0,
                             jnp.where(key == thr, jnp.where(k0 + row_iota_a <= j_eff, 0.0, NEG), NEG))

        attend(bias_fn)


def _dsa_call(qaT, qiT, smallT, gates, knat, vT, *, tk, ta):
    B, _, T = qaT.shape
    assert tk == 32 * 8, "a key tile is transposed as 32 vregs of 8 rows"
    qblk = lambda r: pl.BlockSpec((1, r, Q_BLOCK), lambda b, i: (b, 0, i))
    return pl.pallas_call(
        functools.partial(_dsa_kernel, tk=tk, ta=ta, ts=min(2 * tk, T), topk=min(DSA_TOPK, T // 4),
                          jbits=T.bit_length()),
        grid=(B, T // Q_BLOCK),
        in_specs=[qblk(256), qblk(256), qblk(64),
                  pl.BlockSpec((1, Q_BLOCK, 256), lambda b, i: (b, i, 0)),
                  pl.BlockSpec((1, T, 256), lambda b, i: (b, 0, 0)),
                  pl.BlockSpec((1, 192, T), lambda b, i: (b, 0, 0))],
        out_specs=pl.BlockSpec((1, Q_BLOCK, 256), lambda b, i: (b, i, 0)),
        out_shape=jax.ShapeDtypeStruct((B, T, 256), BF16),
        scratch_shapes=[pltpu.VMEM((T, Q_BLOCK), I32), pltpu.VMEM((T // tk, 32, 8, Q_BLOCK), I32),
                        pltpu.VMEM((8, Q_BLOCK), I32),
                        pltpu.VMEM((2, tk, IDX_HEADS * Q_BLOCK), F32)] + _attention_scratch(ta),
        compiler_params=pltpu.CompilerParams(
            dimension_semantics=("parallel", "arbitrary"), vmem_limit_bytes=VMEM_LIMIT),
        name="dsa",
    )(qaT, qiT, smallT, gates, knat, vT)


def _nsa_kernel(qbT_ref, smallT_ref, gb_ref, knat_ref, vT_ref, kcmp_ref, vcmpT_ref, ovT_ref,
                out_ref, sel_ref, s_ref, p_ref, acc_ref, sw_ref, pw_ref, accw_ref, *, ta, seq):
    i = pl.program_id(1)
    q0 = i * Q_BLOCK
    ncp = seq // CMP_STRIDE
    nslc = seq // SLC_BLOCK
    topn = min(SLC_TOPN, nslc)
    t_row = q0 + lax.broadcasted_iota(I32, (1, Q_BLOCK), 1)
    qT = _heads_on_lanes(qbT_ref)

    s_c = _dot(kcmp_ref[0], qT)
    n_iota = lax.broadcasted_iota(I32, (ncp, Q_BLOCK), 0)
    cend = n_iota * CMP_STRIDE + (CMP_LEN - 1)
    sm = s_c + _rep_heads(jnp.where(cend <= t_row, 0.0, NEG))
    p_c = jnp.exp2(sm - jnp.maximum(jnp.max(sm, axis=0, keepdims=True), M_INIT))
    p_c = p_c / jnp.maximum(jnp.sum(p_c, axis=0, keepdims=True), 1e-30)
    o_cmp = _dot(vcmpT_ref[0], p_c.astype(BF16))

    psum = p_c[:, 0:Q_BLOCK]
    for hh in range(1, N_HEADS):
        psum = psum + p_c[:, Q_BLOCK * hh:Q_BLOCK * (hh + 1)]
    imp = jnp.dot(ovT_ref[...], psum, preferred_element_type=F32,
                  precision=lax.Precision.HIGHEST)
    jidx = lax.broadcasted_iota(I32, (nslc, Q_BLOCK), 0)
    blk_t = t_row // SLC_BLOCK
    v = jnp.where(jidx == 0, jnp.inf,
                  jnp.where(jidx == blk_t, jnp.inf, jnp.where(jidx <= blk_t, imp, -jnp.inf)))
    sub = lax.broadcasted_iota(I32, (8, Q_BLOCK), 0)
    vg = [v[8 * g:8 * (g + 1), :] for g in range(nslc // 8)]
    rank = [jnp.zeros((8, Q_BLOCK), I32) for _ in vg]
    for jp in range(nslc):
        r = v[jp:jp + 1, :]
        for g in range(len(vg)):
            if g > jp // 8:
                beats = jnp.where(r >= vg[g], 1, 0)
            elif g < jp // 8:
                beats = jnp.where(r > vg[g], 1, 0)
            else:
                beats = jnp.where(r > vg[g], 1,
                                  jnp.where(r == vg[g], jnp.where(sub > jp % 8, 1, 0), 0))
            rank[g] = rank[g] + beats
    rank = jnp.concatenate(rank, axis=0)
    sel_ref[...] = jnp.where(rank < topn, jnp.where(jidx <= blk_t, 0.0, NEG), NEG)

    row_a = lax.broadcasted_iota(I32, (ta, Q_BLOCK), 0)
    n_ta = (q0 + Q_BLOCK + ta - 1) // ta

    def slc_bias(k0):
        j0 = k0 // SLC_BLOCK
        rows = [jnp.broadcast_to(sel_ref[pl.ds(j0 + jj, 1), :], (SLC_BLOCK, Q_BLOCK))
                for jj in range(ta // SLC_BLOCK)]
        return jnp.where(k0 + row_a <= t_row, jnp.concatenate(rows, axis=0), NEG)

    o_slc = _masked_attention(
        qT,
        lambda k0: knat_ref[0, pl.ds(k0, ta), HEAD_DIM:2 * HEAD_DIM],
        lambda k0: vT_ref[0, HEAD_DIM:2 * HEAD_DIM, pl.ds(k0, ta)],
        slc_bias, 0, n_ta, ta, s_ref, p_ref, acc_ref)

    def win_bias(k0):
        pos = k0 + row_a
        return jnp.where(pos <= t_row, jnp.where(t_row - pos < WINDOW, 0.0, NEG), NEG)

    o_win = _masked_attention(
        qT,
        lambda k0: knat_ref[0, pl.ds(k0, ta), 2 * HEAD_DIM:3 * HEAD_DIM],
        lambda k0: vT_ref[0, 2 * HEAD_DIM:3 * HEAD_DIM, pl.ds(k0, ta)],
        win_bias, jnp.maximum((q0 - WINDOW) // ta, 0), n_ta, ta, sw_ref, pw_ref, accw_ref)

    g = 1.0 / (1.0 + jnp.exp(-smallT_ref[0, IDX_HEADS:IDX_HEADS + 3 * N_HEADS, :]))
    parts = []
    for hh in range(N_HEADS):
        cols = slice(Q_BLOCK * hh, Q_BLOCK * (hh + 1))
        parts.append(g[3 * hh:3 * hh + 1, :] * o_cmp[:, cols]
                     + g[3 * hh + 1:3 * hh + 2, :] * o_slc[:, cols]
                     + g[3 * hh + 2:3 * hh + 3, :] * o_win[:, cols])
    o_tok = jnp.concatenate(parts, axis=0).T
    out_ref[0] = (o_tok * gb_ref[0]).astype(BF16)


def _nsa_call(qbT, smallT, gates, knat, vT, kcmp, vcmpT, ovT, *, ta):
    B, _, T = qbT.shape
    ncp, nslc = T // CMP_STRIDE, T // SLC_BLOCK
    qblk = lambda r: pl.BlockSpec((1, r, Q_BLOCK), lambda b, i: (b, 0, i))
    return pl.pallas_call(
        functools.partial(_nsa_kernel, ta=ta, seq=T),
        grid=(B, T // Q_BLOCK),
        in_specs=[qblk(256), qblk(64),
                  pl.BlockSpec((1, Q_BLOCK, 256), lambda b, i: (b, i, 1)),
                  pl.BlockSpec((1, T, 256), lambda b, i: (b, 0, 0)),
                  pl.BlockSpec((1, 192, T), lambda b, i: (b, 0, 0)),
                  pl.BlockSpec((1, ncp, HEAD_DIM), lambda b, i: (b, 0, 0)),
                  pl.BlockSpec((1, HEAD_DIM, ncp), lambda b, i: (b, 0, 0)),
                  pl.BlockSpec((nslc, ncp), lambda b, i: (0, 0))],
        out_specs=pl.BlockSpec((1, Q_BLOCK, 256), lambda b, i: (b, i, 0)),
        out_shape=jax.ShapeDtypeStruct((B, T, 256), BF16),
        scratch_shapes=[pltpu.VMEM((nslc, Q_BLOCK), F32)] + _attention_scratch(ta)
        + _attention_scratch(ta),
        compiler_params=pltpu.CompilerParams(
            dimension_semantics=("parallel", "arbitrary"), vmem_limit_bytes=VMEM_LIMIT),
        name="nsa",
    )(qbT, smallT, gates, knat, vT, kcmp, vcmpT, ovT)


def _out_kernel(x_ref, ma_ref, mb_ref, mcd_ref, wo_ref, fw_ref, o_ref):
    y = (x_ref[...] + _dot(ma_ref[...], wo_ref[0:256, :]) + _dot(mb_ref[...], wo_ref[256:512, :])
         + _dot(mcd_ref[...], wo_ref[512:1024, :]))
    ms = jnp.mean(y * y, axis=-1, keepdims=True)
    o_ref[...] = y * lax.rsqrt(ms + EPS) * fw_ref[...]


def _out_call(x2, ma, mb, mcd, wo, fw, *, tm):
    R, D = x2.shape
    rows = lambda w: pl.BlockSpec((tm, w), lambda r: (r, 0))
    return pl.pallas_call(
        _out_kernel,
        grid=(R // tm,),
        in_specs=[rows(D), rows(256), rows(256), rows(512),
                  pl.BlockSpec((D, D), lambda r: (0, 0)), pl.BlockSpec((1, D), lambda r: (0, 0))],
        out_specs=rows(D),
        out_shape=jax.ShapeDtypeStruct((R, D), F32),
        compiler_params=pltpu.CompilerParams(
            dimension_semantics=("parallel",), vmem_limit_bytes=VMEM_LIMIT),
        name="outproj",
    )(x2, ma, mb, mcd, wo, fw)


def _rope_angles(pos, rot):
    half = rot // 2
    inv = ROPE_THETA ** (-jnp.arange(half, dtype=F32) / half)
    ang = pos.astype(F32)[:, None] * inv[None, :]
    return jnp.cos(ang), jnp.sin(ang)


def _rope_tables(T):
    pos = jnp.arange(T)
    c16, s16 = _rope_angles(pos, ROT_DIM)
    c8, s8 = _rope_angles(pos, IDX_ROT)
    cat = lambda parts: jnp.concatenate(parts, axis=1)
    z = lambda w, n=T: jnp.zeros((n, w), F32)
    o = lambda w, n=T: jnp.ones((n, w), F32)
    C = cat([c16, c16, o(48)] * 3 + [c8, c8, o(56)])
    Sa = cat([-s16, z(56)] * 3 + [z(64)])
    Sb = cat([z(8), s16, z(48)] * 3 + [z(64)])
    Sc = cat([z(KI_OFF), -s8, z(60)])
    Sd = cat([z(KI_OFF + 4), s8, z(56)])
    tabn = jnp.stack([C, Sa, Sb, Sc, Sd])
    tabt = jnp.stack([c16.T, s16.T, jnp.concatenate([c8.T, c8.T]),
                      jnp.concatenate([-s8.T, s8.T])])
    ncp = T // CMP_STRIDE
    cc, sc_ = _rope_angles(jnp.arange(ncp) * CMP_STRIDE + CMP_LEN - 1, ROT_DIM)
    tabc = jnp.stack([cat([cc, cc, o(112, ncp)]), cat([-sc_, z(120, ncp)]),
                      cat([z(8, ncp), sc_, z(112, ncp)])])
    return tabn, tabt, tabc


def _split_offsets():
    widths = (("a_q", 256), ("a_k", 64), ("a_v", 64), ("a_qi", 256), ("a_ki", 32), ("a_wi", 8),
              ("a_gate", 256), ("b_q", 256), ("b_kc", 64), ("b_vc", 64), ("b_ks", 64),
              ("b_vs", 64), ("b_kw", 64), ("b_vw", 64), ("b_g", 12), ("b_gate", 256),
              ("c_b", 256), ("c_c", 256), ("c_x", 256), ("c_gate", 256), ("d_u", 256),
              ("d_gate", 256))
    offs, o = {}, 0
    for name, w in widths:
        offs[name] = (o, o + w)
        o += w
    return offs


def _relaid_weights(w_in, pool_w, pe_cmp, w_cmp_k, w_cmp_v):
    offs = _split_offsets()
    col = lambda n: w_in[:, :, offs[n][0]:offs[n][1]]
    L, D, _ = w_in.shape
    wn = jnp.concatenate(
        [col("a_k"), col("b_ks"), col("b_kw"), col("a_ki"), jnp.zeros((L, D, 32), F32),
         col("b_kc"), col("b_vc"), col("a_gate"), col("b_gate"), col("c_b"), col("c_c"),
         col("c_x"), col("c_gate"), col("d_u"), col("d_gate")], axis=2).astype(BF16)
    qscale = HEAD_DIM ** -0.5 * LOG2E
    iscale = (IDX_DIM ** -0.5) * (IDX_HEADS ** -0.5)
    wt = jnp.swapaxes(jnp.concatenate(
        [col("a_q") * qscale, col("a_qi"), col("b_q") * qscale, col("a_v"), col("b_vs"),
         col("b_vw"), col("a_wi") * iscale, col("b_g"), jnp.zeros((L, D, 44), F32)],
        axis=2), 1, 2).astype(BF16)
    groups, pch = pool_w.shape[1], pool_w.shape[2]
    poolw = jnp.concatenate(
        [jnp.concatenate([jnp.zeros((L, pch, g * pch), F32), pool_w[:, g],
                          jnp.zeros((L, pch, (groups - 1 - g) * pch), F32)], axis=2)
         for g in range(groups)], axis=1).astype(BF16)
    half = CMP_LEN // 2
    wk4 = w_cmp_k.reshape(L, CMP_LEN, HEAD_DIM, HEAD_DIM)
    wv4 = w_cmp_v.reshape(L, CMP_LEN, HEAD_DIM, HEAD_DIM)
    zk = jnp.zeros((L, half, HEAD_DIM, HEAD_DIM), F32)

    def chunk_w(lo):
        top = jnp.concatenate([wk4[:, lo:lo + half], zk], axis=3)
        bot = jnp.concatenate([zk, wv4[:, lo:lo + half]], axis=3)
        return jnp.concatenate([top, bot], axis=2).reshape(L, half * 128, 128)

    wc = jnp.stack([chunk_w(0), chunk_w(half)], axis=1).astype(BF16)
    pe_rows = lambda lo: jnp.concatenate([pe_cmp[:, lo:lo + half]] * 2, axis=2).reshape(L, 1, half * 128)
    pe2 = jnp.stack([pe_rows(0), pe_rows(half)], axis=1)
    return wn, wt, poolw, wc, pe2


def _overlap_T(T):
    ncp, nslc = T // CMP_STRIDE, T // SLC_BLOCK
    n_cmp = (T - CMP_LEN) // CMP_STRIDE + 1
    cs = np.arange(ncp) * CMP_STRIDE
    ce = cs + CMP_LEN - 1
    ss = np.arange(nslc) * SLC_BLOCK
    ov = (cs[None, :] < ss[:, None] + SLC_BLOCK) & (ce[None, :] >= ss[:, None]) & (np.arange(ncp)[None, :] < n_cmp)
    return jnp.asarray(ov.astype(np.float32))


def kernel(x, norm_w, w_in, w_out, conv_w, pe_cmp, w_cmp_k, w_cmp_v, pool_w, pool_scale, final_norm_w):
    B, T, D = x.shape
    depth = w_in.shape[0]
    tm = min(512, T)
    sub = min(256, tm)
    tk = min(256, T)
    ta = min(128, T)
    tabn, tabt, tabc = _rope_tables(T)
    ovT = _overlap_T(T)
    fw = final_norm_w.reshape(1, D)
    wn, wt, poolw, wc, pe2 = _relaid_weights(w_in, pool_w, pe_cmp, w_cmp_k, w_cmp_v)
    wo = w_out.astype(BF16)
    prev = None
    for l in range(depth):
        outs = _proj_call(x, prev, norm_w[l].reshape(1, D), wn[l], wt[l], tabn, tabt, conv_w[l], poolw[l],
                          pool_scale[l].reshape(1, W_GROUP), tm=tm, sub=sub)
        if prev is not None:
            x, outs = outs[0], outs[1:]
        knat, kvc, gates, mcd, qaT, qiT, qbT, vT, smallT = outs
        kcmp, vcmpT = _cmp_call(kvc.reshape(B, T // CMP_STRIDE, CMP_STRIDE * 128), pe2[l], wc[l], tabc)
        ma = _dsa_call(qaT, qiT, smallT, gates, knat, vT, tk=tk, ta=ta)
        mb = _nsa_call(qbT, smallT, gates, knat, vT, kcmp, vcmpT, ovT, ta=ta)
        prev = (ma, mb, mcd, wo[l])
    x = _out_call(x.reshape(B * T, D), ma.reshape(B * T, 256), mb.reshape(B * T, 256),
                  mcd.reshape(B * T, 512), wo[depth - 1], fw, tm=tm).reshape(B, T, D)
    return x
```

```python
import functools

import numpy as np
import jax
import jax.numpy as jnp
from jax import lax
from jax.experimental import pallas as pl
from jax.experimental.pallas import tpu as pltpu

HEAD_DIM = 64
N_HEADS = 4
ROT_DIM = HEAD_DIM // 4
ROPE_THETA = 500000.0
EPS = 1e-6
IDX_HEADS = 8
IDX_DIM = 32
IDX_ROT = IDX_DIM // 4
DSA_TOPK = 256
CMP_LEN = 32
CMP_STRIDE = 16
SLC_BLOCK = 64
SLC_TOPN = 16
WINDOW = 512
CONV_WIDTH = 3
POOL_WINDOWS = (2, 4, 8, 16)
W_GROUP = 256

Q_BLOCK = 256
HALO = 16
NEG = -1e30
M_INIT = -1e29
DEN_ROWS = 16
LOG2E = 1.4426950408889634
VMEM_LIMIT = 48 * 1024 * 1024

F32 = jnp.float32
BF16 = jnp.bfloat16
I32 = jnp.int32
INT_MIN = -2 ** 31

N_K = 0
N_KVC = 256
N_GATE = 384
N_CB, N_CC, N_CX, N_CG, N_DU, N_DG = 896, 1152, 1408, 1664, 1920, 2176
N_NAT = 2432
T_QA, T_QI, T_QB, T_V, T_SMALL, N_TR = 0, 256, 512, 768, 960, 1024
KI_OFF = 192


def _silu(v):
    return v / (1.0 + jnp.exp(-v))


def _dot(a, b):
    return jnp.dot(a, b, preferred_element_type=F32)


def _proj_kernel(*refs, tm, sub, fused):
    if fused:
        x_ref, ma_ref, mb_ref, mcdin_ref, wo_ref = refs[:5]
        refs = refs[5:]
    else:
        x_ref = refs[0]
        refs = refs[1:]
    nw_ref, wn_ref, wt_ref, tabn_ref, tabt_ref, convw_ref, poolw_ref, pscale_ref = refs[:8]
    refs = refs[8:]
    if fused:
        xo_ref = refs[0]
        refs = refs[1:]
    knat_o, kvc_o, gates_o, mcd_o, qaT_o, qiT_o, qbT_o, vT_o, smallT_o, cu_ref, du_ref = refs
    t_idx = pl.program_id(1)

    @pl.when(t_idx == 0)
    def _():
        cu_ref[0:HALO, :] = jnp.zeros((HALO, 256), F32)
        du_ref[0:HALO, :] = jnp.zeros((HALO, 256), F32)

    @pl.when(t_idx > 0)
    def _():
        cu_ref[0:HALO, :] = cu_ref[tm:tm + HALO, :]
        du_ref[0:HALO, :] = du_ref[tm:tm + HALO, :]

    lane = lax.broadcasted_iota(I32, (sub, 256), 1)
    row = lax.broadcasted_iota(I32, (sub, 256), 0)
    pch = W_GROUP // len(POOL_WINDOWS)
    cw = convw_ref[...]

    for r0 in range(0, tm, sub):
        rows = slice(r0, r0 + sub)
        x = x_ref[0, rows, :]
        if fused:
            x = (x + _dot(ma_ref[0, rows, :], wo_ref[0:256, :]) + _dot(mb_ref[0, rows, :], wo_ref[256:512, :])
                 + _dot(mcdin_ref[0, rows, :], wo_ref[512:1024, :]))
            xo_ref[0, rows, :] = x
        ms = jnp.mean(x * x, axis=-1, keepdims=True)
        h = (x * lax.rsqrt(ms + EPS) * nw_ref[...]).astype(BF16)
        zn = _dot(h, wn_ref[...])
        zt = lax.dot_general(wt_ref[...], h, (((1,), (1,)), ((), ())),
                             preferred_element_type=F32)

        zk = zn[:, N_K:N_K + 256]
        kr = (zk * tabn_ref[0, rows, :]
              + pltpu.roll(zk, 256 - 8, 1) * tabn_ref[1, rows, :] + pltpu.roll(zk, 8, 1) * tabn_ref[2, rows, :]
              + pltpu.roll(zk, 256 - 4, 1) * tabn_ref[3, rows, :] + pltpu.roll(zk, 4, 1) * tabn_ref[4, rows, :])
        knat_o[0, rows, :] = kr.astype(BF16)
        kvc_o[0, rows, :] = zn[:, N_KVC:N_KVC + 128]
        gates_o[0, rows, :] = _silu(zn[:, N_GATE:N_GATE + 512])

        c_b = zn[:, N_CB:N_CB + 256]
        u = zn[:, N_CC:N_CC + 256] * zn[:, N_CX:N_CX + 256]
        d_u = zn[:, N_DU:N_DU + 256]
        base = HALO + r0
        cu_ref[base:base + sub, :] = u
        du_ref[base:base + sub, :] = d_u
        y = (cu_ref[base - 2:base - 2 + sub, :] * cw[0:1, :]
             + cu_ref[base - 1:base - 1 + sub, :] * cw[1:2, :] + u * cw[2:3, :])
        o_c = c_b * y

        acc = d_u
        sums = {}
        for k in range(1, max(POOL_WINDOWS)):
            acc = acc + du_ref[base - k:base - k + sub, :]
            if k + 1 in POOL_WINDOWS:
                sums[k + 1] = acc
        ssel = sums[POOL_WINDOWS[-1]]
        wl = jnp.full((sub, 256), POOL_WINDOWS[-1], I32)
        for g in range(len(POOL_WINDOWS) - 2, -1, -1):
            ssel = jnp.where(lane < (g + 1) * pch, sums[POOL_WINDOWS[g]], ssel)
            wl = jnp.where(lane < (g + 1) * pch, POOL_WINDOWS[g], wl)
        cnt = jnp.minimum(t_idx * tm + r0 + row + 1, wl).astype(F32)
        pooled = ssel / cnt - d_u
        o_d = _dot(pooled.astype(BF16), poolw_ref[...]) * pscale_ref[...]
        mcd_o[0, rows, :] = jnp.concatenate(
            [_silu(zn[:, N_CG:N_CG + 256]) * o_c, _silu(zn[:, N_DG:N_DG + 256]) * o_d],
            axis=1).astype(BF16)

        c16, s16 = tabt_ref[0, :, rows], tabt_ref[1, :, rows]
        c8, s8 = tabt_ref[2, :, rows], tabt_ref[3, :, rows]

        def rope_heads(base_row, out_ref):
            for hh in range(N_HEADS):
                b = base_row + HEAD_DIM * hh
                x1, x2 = zt[b:b + 8, :], zt[b + 8:b + 16, :]
                o = jnp.concatenate([x1 * c16 - x2 * s16, x2 * c16 + x1 * s16], axis=0)
                out_ref[0, HEAD_DIM * hh:HEAD_DIM * hh + 16, rows] = o.astype(BF16)
                out_ref[0, HEAD_DIM * hh + 16:HEAD_DIM * (hh + 1), rows] = zt[b + 16:b + HEAD_DIM, :].astype(BF16)

        rope_heads(T_QA, qaT_o)
        rope_heads(T_QB, qbT_o)
        for hh in range(IDX_HEADS):
            b = T_QI + IDX_DIM * hh
            x8 = zt[b:b + 8, :]
            o = x8 * c8 + pltpu.roll(x8, 4, 0) * s8
            qiT_o[0, IDX_DIM * hh:IDX_DIM * hh + 16, rows] = jnp.concatenate(
                [o, zt[b + 8:b + 16, :]], axis=0).astype(BF16)
            qiT_o[0, IDX_DIM * hh + 16:IDX_DIM * (hh + 1), rows] = zt[b + 16:b + IDX_DIM, :].astype(BF16)
        vT_o[0, :, rows] = zt[T_V:T_V + 192, :].astype(BF16)
        smallT_o[0, :, rows] = zt[T_SMALL:T_SMALL + 64, :]


def _proj_call(x, prev, nw, wn, wt, tabn, tabt, convw, poolw, pscale, *, tm, sub):
    B, T, D = x.shape
    grid = (B, T // tm)
    const = lambda *shape: pl.BlockSpec(shape, lambda b, t: (0,) * len(shape))
    nat = lambda w: pl.BlockSpec((1, tm, w), lambda b, t: (b, t, 0))
    tr = lambda r: pl.BlockSpec((1, r, tm), lambda b, t: (b, 0, t))
    fused = prev is not None
    out_shape = (
        jax.ShapeDtypeStruct((B, T, 256), BF16), jax.ShapeDtypeStruct((B, T, 128), F32),
        jax.ShapeDtypeStruct((B, T, 512), F32), jax.ShapeDtypeStruct((B, T, 512), BF16),
        jax.ShapeDtypeStruct((B, 256, T), BF16), jax.ShapeDtypeStruct((B, 256, T), BF16),
        jax.ShapeDtypeStruct((B, 256, T), BF16), jax.ShapeDtypeStruct((B, 192, T), BF16),
        jax.ShapeDtypeStruct((B, 64, T), F32))
    out_specs = (nat(256), nat(128), nat(512), nat(512), tr(256), tr(256), tr(256), tr(192), tr(64))
    in_specs = [const(1, D), const(D, N_NAT), const(N_TR, D),
                pl.BlockSpec((5, tm, 256), lambda b, t: (0, t, 0)),
                pl.BlockSpec((4, 8, tm), lambda b, t: (0, 0, t)),
                const(CONV_WIDTH, 256), const(256, 256), const(1, 256)]
    args = (nw, wn, wt, tabn, tabt, convw, poolw, pscale)
    if fused:
        in_specs = [nat(D), nat(256), nat(256), nat(512), const(D, D)] + in_specs
        args = (x,) + tuple(prev) + args
        out_shape = (jax.ShapeDtypeStruct((B, T, D), F32),) + out_shape
        out_specs = (nat(D),) + out_specs
    else:
        in_specs = [nat(D)] + in_specs
        args = (x,) + args
    return pl.pallas_call(
        functools.partial(_proj_kernel, tm=tm, sub=sub, fused=fused),
        grid=grid,
        in_specs=in_specs,
        out_specs=out_specs,
        out_shape=out_shape,
        scratch_shapes=[pltpu.VMEM((tm + HALO, 256), F32), pltpu.VMEM((tm + HALO, 256), F32)],
        compiler_params=pltpu.CompilerParams(
            dimension_semantics=("parallel", "arbitrary"), vmem_limit_bytes=VMEM_LIMIT),
        name="proj",
    )(*args)


def _cmp_kernel(kvc_ref, pe_ref, wc_ref, tab_ref, kcmp_o, vcmpT_o, *, ncp):
    c = kvc_ref[0]
    first = _dot((c + pe_ref[0]).astype(BF16), wc_ref[0])
    second = _dot((c + pe_ref[1]).astype(BF16), wc_ref[1])
    kv = first + pltpu.roll(second, ncp - 1, 0)
    kr = (kv * tab_ref[0] + pltpu.roll(kv, 128 - 8, 1) * tab_ref[1]
          + pltpu.roll(kv, 8, 1) * tab_ref[2])
    kcmp_o[0] = kr[:, 0:HEAD_DIM].astype(BF16)
    vcmpT_o[0] = kr.T[HEAD_DIM:2 * HEAD_DIM, :].astype(BF16)


def _cmp_call(kvc_chunks, pe2, wc, tabc):
    B, ncp, width = kvc_chunks.shape
    return pl.pallas_call(
        functools.partial(_cmp_kernel, ncp=ncp),
        grid=(B,),
        in_specs=[pl.BlockSpec((1, ncp, width), lambda b: (b, 0, 0)),
                  pl.BlockSpec((2, 1, width), lambda b: (0, 0, 0)),
                  pl.BlockSpec((2, width, 128), lambda b: (0, 0, 0)),
                  pl.BlockSpec((3, ncp, 128), lambda b: (0, 0, 0))],
        out_specs=(pl.BlockSpec((1, ncp, HEAD_DIM), lambda b: (b, 0, 0)),
                   pl.BlockSpec((1, HEAD_DIM, ncp), lambda b: (b, 0, 0))),
        out_shape=(jax.ShapeDtypeStruct((B, ncp, HEAD_DIM), BF16),
                   jax.ShapeDtypeStruct((B, HEAD_DIM, ncp), BF16)),
        compiler_params=pltpu.CompilerParams(
            dimension_semantics=("parallel",), vmem_limit_bytes=VMEM_LIMIT),
        name="compress",
    )(kvc_chunks, pe2, wc, tabc)


def _heads_on_lanes(qT_ref):
    return jnp.concatenate(
        [qT_ref[0, HEAD_DIM * hh:HEAD_DIM * (hh + 1), :] for hh in range(N_HEADS)], axis=1)


def _rep_heads(a):
    return jnp.concatenate([a] * N_HEADS, axis=1)


def _tree_sum(xs):
    xs = list(xs)
    while len(xs) > 1:
        xs = [xs[a] + xs[a + 1] for a in range(0, len(xs) - 1, 2)] + ([xs[-1]] if len(xs) % 2 else [])
    return xs[0]


def _masked_attention(qT, k_load, vT_load, bias_fn, lo, hi, tk, s_ref, p_ref, acc_ref):
    width = N_HEADS * Q_BLOCK
    last = hi - 1

    def tile_start(kt):
        return pl.multiple_of(jnp.minimum(kt, last) * tk, tk)

    def scores(kt):
        return _dot(k_load(tile_start(kt)), qT)

    ones_rows = jnp.ones((DEN_ROWS, tk), BF16)

    def values(kt, slot, alpha):
        vT1 = jnp.concatenate([vT_load(tile_start(kt)), ones_rows], axis=0)
        acc_ref[...] = alpha * acc_ref[...] + _dot(vT1, p_ref[slot])

    def step(kt, slot, carry):
        m, alpha_prev = carry
        values(jnp.maximum(kt - 1, lo), 1 - slot, alpha_prev)
        s = s_ref[slot]
        s_ref[1 - slot] = scores(kt + 1)
        bias = jnp.where(kt <= last, bias_fn(tile_start(kt)), NEG)
        sm = s + _rep_heads(bias)
        m_new = jnp.maximum(m, jnp.max(sm, axis=0, keepdims=True))
        p_ref[slot] = jnp.exp2(sm - m_new).astype(BF16)
        return m_new, jnp.exp2(m - m_new)

    s_ref[0] = scores(lo)
    p_ref[1] = jnp.zeros((tk, width), BF16)
    acc_ref[...] = jnp.zeros((HEAD_DIM + DEN_ROWS, width), F32)

    def body(j, carry):
        kt = lo + 2 * j
        return step(kt + 1, 1, step(kt, 0, carry))

    init = (jnp.full((1, width), M_INIT, F32), jnp.ones((1, width), F32))
    trips = (hi - lo + 1) // 2
    _, alpha = lax.fori_loop(0, trips, body, init)
    values(lo + 2 * trips - 1, 1, alpha)
    return acc_ref[0:HEAD_DIM, :] / jnp.maximum(acc_ref[HEAD_DIM:HEAD_DIM + 1, :], 1e-30)


def _attention_scratch(tk):
    width = N_HEADS * Q_BLOCK
    return [pltpu.VMEM((2, tk, width), F32), pltpu.VMEM((2, tk, width), BF16),
            pltpu.VMEM((HEAD_DIM + DEN_ROWS, width), F32)]


def _to_token_major(oT):
    stacked = jnp.concatenate(
        [oT[:, Q_BLOCK * hh:Q_BLOCK * (hh + 1)] for hh in range(N_HEADS)], axis=0)
    return stacked.T


_SWAP_MASK = {16: 0x0000FFFF, 8: 0x00FF00FF, 4: 0x0F0F0F0F, 2: 0x33333333, 1: 0x55555555}


def _bit_swap(words, lo, j):
    t = (words[lo] ^ (words[lo + j] >> j)) & _SWAP_MASK[j]
    words[lo] = words[lo] ^ t
    words[lo + j] = words[lo + j] ^ (t << j)


def _dsa_kernel(qaT_ref, qiT_ref, smallT_ref, ga_ref, knat_ref, vT_ref, out_ref,
                keys_ref, planes_ref, res_ref, lg_ref, s_ref, p_ref, acc_ref, *, tk, ta, ts, topk, jbits):
    i = pl.program_id(1)

    @pl.when(i == 0)
    def _():
        planes_ref[...] = jnp.zeros(planes_ref.shape, I32)

    q0 = i * Q_BLOCK
    n_kt = (q0 + Q_BLOCK + tk - 1) // tk
    n_st = (q0 + Q_BLOCK + ts - 1) // ts
    t_row = q0 + lax.broadcasted_iota(I32, (1, Q_BLOCK), 1)
    row_iota_s = lax.broadcasted_iota(I32, (ts, Q_BLOCK), 0)

    qi_cat = jnp.concatenate(
        [qiT_ref[0, IDX_DIM * hh:IDX_DIM * (hh + 1), :] for hh in range(IDX_HEADS)], axis=1)
    wi = smallT_ref[0, 0:IDX_HEADS, :]

    def tile_start(kt):
        return pl.multiple_of(jnp.minimum(kt, n_kt - 1) * tk, tk)

    def logits(kt):
        return _dot(knat_ref[0, pl.ds(tile_start(kt), tk), KI_OFF:KI_OFF + IDX_DIM], qi_cat)

    lg_ref[0] = logits(0)

    wi_rows = [jnp.broadcast_to(wi[hh:hh + 1, :], (8, Q_BLOCK)) for hh in range(IDX_HEADS)]
    row8 = lax.broadcasted_iota(I32, (8, Q_BLOCK), 0)

    def slab_key(slot, k0, a):
        sc = jnp.zeros((8, Q_BLOCK), F32)
        for hh in range(IDX_HEADS):
            lg = lg_ref[slot, 8 * a:8 * (a + 1), Q_BLOCK * hh:Q_BLOCK * (hh + 1)]
            sc = sc + jnp.maximum(lg, 0.0) * wi_rows[hh]
        sc = jnp.where(sc == 0.0, 0.0, sc)
        bits = lax.bitcast_convert_type(sc, I32)
        key = bits ^ ((bits >> 31) & 0x7FFFFFFF)
        key = jnp.where(k0 + 8 * a + row8 <= t_row, key, INT_MIN)
        keys_ref[pl.ds(k0 + 8 * a, 8), :] = key
        return key

    def score_step(kt, slot):
        k0 = tile_start(kt)
        tile = k0 // tk
        lg_ref[1 - slot] = logits(kt + 1)
        for q in range(8):
            words = {a: slab_key(slot, k0, a) for a in (q, q + 8, q + 16, q + 24)}
            _bit_swap(words, q, 16)
            _bit_swap(words, q + 8, 16)
            _bit_swap(words, q, 8)
            _bit_swap(words, q + 16, 8)
            for a, w in words.items():
                planes_ref[tile, a] = w
        for g in range(0, 32, 8):
            words = {a: planes_ref[tile, a] for a in range(g, g + 8)}
            for j in (4, 2, 1):
                for lo in range(g, g + 8):
                    if not lo & j:
                        _bit_swap(words, lo, j)
            for a, w in words.items():
                planes_ref[tile, a] = ~w if a == 0 else w

    def score_body(kt, carry):
        for slot in (0, 1):
            pl.when(kt % 2 == slot)(functools.partial(score_step, kt, slot))
        return carry

    lax.fori_loop(0, n_kt, score_body, 0)

    def search(nt):
        def bit_step(b, carry):
            alive, c_above, thr_u = carry
            ones = [alive[a] & planes_ref[a, b] for a in range(nt)]
            cnt1 = jnp.sum(_tree_sum([lax.population_count(o) for o in ones]), axis=0, keepdims=True)
            take1 = c_above + cnt1 >= topk
            alive = tuple(jnp.where(take1, o, al ^ o) for al, o in zip(alive, ones))
            bit = lax.shift_left(jnp.int32(1), 31 - b)
            return alive, jnp.where(take1, c_above, c_above + cnt1), jnp.where(take1, thr_u | bit, thr_u)

        alive0 = tuple(jnp.where(a < n_kt, jnp.full((8, Q_BLOCK), -1, I32), 0) for a in range(nt))
        zero_row = jnp.zeros((1, Q_BLOCK), I32)
        alive, c_above, thr_u = lax.fori_loop(0, 32, bit_step, (alive0, zero_row, zero_row))
        n_eq = jnp.sum(_tree_sum([lax.population_count(al) for al in alive]), axis=0, keepdims=True)
        res_ref[0:1, :] = thr_u ^ INT_MIN
        res_ref[1:2, :] = c_above
        res_ref[2:3, :] = c_above + n_eq

    n_tiles = planes_ref.shape[0]
    sizes = tuple(range(4, n_tiles + 1, 4)) if n_tiles % 4 == 0 else (n_tiles,)
    for below, nt in zip((0,) + sizes, sizes):
        pl.when((n_kt > below) & (n_kt <= nt))(functools.partial(search, nt))
    thr, c_above, cnt_ge = res_ref[0:1, :], res_ref[1:2, :], res_ref[2:3, :]

    def pad_body(kt, carry):
        keys_ref[pl.ds(pl.multiple_of(kt * tk, tk), tk), :] = jnp.full((tk, Q_BLOCK), INT_MIN, I32)
        return carry

    def count(pred):
        def body(st, acc):
            k0 = pl.multiple_of(st * ts, ts)
            hit = pred(keys_ref[pl.ds(k0, ts), :], k0)
            return acc + jnp.sum(hit.reshape(ts // 8, 8, Q_BLOCK), axis=0)
        acc = lax.fori_loop(0, n_st, body, jnp.zeros((8, Q_BLOCK), I32))
        return jnp.sum(acc, axis=0, keepdims=True)

    def attend(bias_fn):
        oT = _masked_attention(
            _heads_on_lanes(qaT_ref),
            lambda k0: knat_ref[0, pl.ds(k0, ta), 0:HEAD_DIM],
            lambda k0: vT_ref[0, 0:HEAD_DIM, pl.ds(k0, ta)],
            bias_fn, 0, (q0 + Q_BLOCK + ta - 1) // ta, ta, s_ref, p_ref, acc_ref)
        out_ref[0] = (_to_token_major(oT) * ga_ref[0]).astype(BF16)

    ties = (jnp.max(cnt_ge) > topk) | (i == 0)

    @pl.when(jnp.logical_not(ties))
    def _():
        attend(lambda k0: jnp.where(keys_ref[pl.ds(k0, ta), :] >= thr, 0.0, NEG))

    @pl.when(ties)
    def _():
        lax.fori_loop(n_kt, n_st * (ts // tk), pad_body, 0)
        need = topk - c_above

        def j_body(it, jp):
            cand = jp | lax.shift_left(jnp.int32(1), jbits - 1 - it)
            before = count(lambda blk, k0: jnp.where(
                blk == thr, jnp.where(k0 + row_iota_s < cand, 1, 0), 0))
            return jnp.where(before < need, cand, jp)

        j_eff = jnp.minimum(lax.fori_loop(0, jbits, j_body, jnp.zeros((1, Q_BLOCK), I32)), t_row)
        row_iota_a = lax.broadcasted_iota(I32, (ta, Q_BLOCK), 0)

        def bias_fn(k0):
            key = keys_ref[pl.ds(k0, ta), :]
            return jnp.where(key > thr, 0.0,
                             jnp.where(key == thr, jnp.where(k0 + row_iota_a <= j_eff, 0.0, NEG), NEG))

        attend(bias_fn)


def _dsa_call(qaT, qiT, smallT, gates, knat, vT, *, tk, ta):
    B, _, T = qaT.shape
    assert tk == 32 * 8, "a key tile is transposed as 32 vregs of 8 rows"
    qblk = lambda r: pl.BlockSpec((1, r, Q_BLOCK), lambda b, i: (b, 0, i))
    return pl.pallas_call(
        functools.partial(_dsa_kernel, tk=tk, ta=ta, ts=min(2 * tk, T), topk=min(DSA_TOPK, T // 4),
                          jbits=T.bit_length()),
        grid=(B, T // Q_BLOCK),
        in_specs=[qblk(256), qblk(256), qblk(64),
                  pl.BlockSpec((1, Q_BLOCK, 256), lambda b, i: (b, i, 0)),
                  pl.BlockSpec((1, T, 256), lambda b, i: (b, 0, 0)),
                  pl.BlockSpec((1, 192, T), lambda b, i: (b, 0, 0))],
        out_specs=pl.BlockSpec((1, Q_BLOCK, 256), lambda b, i: (b, i, 0)),
        out_shape=jax.ShapeDtypeStruct((B, T, 256), BF16),
        scratch_shapes=[pltpu.VMEM((T, Q_BLOCK), I32), pltpu.VMEM((T // tk, 32, 8, Q_BLOCK), I32),
                        pltpu.VMEM((8, Q_BLOCK), I32),
                        pltpu.VMEM((2, tk, IDX_HEADS * Q_BLOCK), F32)] + _attention_scratch(ta),
        compiler_params=pltpu.CompilerParams(
            dimension_semantics=("parallel", "arbitrary"), vmem_limit_bytes=VMEM_LIMIT),
        name="dsa",
    )(qaT, qiT, smallT, gates, knat, vT)


def _nsa_kernel(qbT_ref, smallT_ref, gb_ref, knat_ref, vT_ref, kcmp_ref, vcmpT_ref, ovT_ref,
                out_ref, sel_ref, s_ref, p_ref, acc_ref, sw_ref, pw_ref, accw_ref, *, ta, seq):
    i = pl.program_id(1)
    q0 = i * Q_BLOCK
    ncp = seq // CMP_STRIDE
    nslc = seq // SLC_BLOCK
    topn = min(SLC_TOPN, nslc)
    t_row = q0 + lax.broadcasted_iota(I32, (1, Q_BLOCK), 1)
    qT = _heads_on_lanes(qbT_ref)

    s_c = _dot(kcmp_ref[0], qT)
    n_iota = lax.broadcasted_iota(I32, (ncp, Q_BLOCK), 0)
    cend = n_iota * CMP_STRIDE + (CMP_LEN - 1)
    sm = s_c + _rep_heads(jnp.where(cend <= t_row, 0.0, NEG))
    p_c = jnp.exp2(sm - jnp.maximum(jnp.max(sm, axis=0, keepdims=True), M_INIT))
    p_c = p_c / jnp.maximum(jnp.sum(p_c, axis=0, keepdims=True), 1e-30)
    o_cmp = _dot(vcmpT_ref[0], p_c.astype(BF16))

    psum = p_c[:, 0:Q_BLOCK]
    for hh in range(1, N_HEADS):
        psum = psum + p_c[:, Q_BLOCK * hh:Q_BLOCK * (hh + 1)]
    imp = jnp.dot(ovT_ref[...], psum, preferred_element_type=F32,
                  precision=lax.Precision.HIGHEST)
    jidx = lax.broadcasted_iota(I32, (nslc, Q_BLOCK), 0)
    blk_t = t_row // SLC_BLOCK
    v = jnp.where(jidx == 0, jnp.inf,
                  jnp.where(jidx == blk_t, jnp.inf, jnp.where(jidx <= blk_t, imp, -jnp.inf)))
    sub = lax.broadcasted_iota(I32, (8, Q_BLOCK), 0)
    vg = [v[8 * g:8 * (g + 1), :] for g in range(nslc // 8)]
    rank = [jnp.zeros((8, Q_BLOCK), I32) for _ in vg]
    for jp in range(nslc):
        r = v[jp:jp + 1, :]
        for g in range(len(vg)):
            if g > jp // 8:
                beats = jnp.where(r >= vg[g], 1, 0)
            elif g < jp // 8:
                beats = jnp.where(r > vg[g], 1, 0)
            else:
                beats = jnp.where(r > vg[g], 1,
                                  jnp.where(r == vg[g], jnp.where(sub > jp % 8, 1, 0), 0))
            rank[g] = rank[g] + beats
    rank = jnp.concatenate(rank, axis=0)
    sel_ref[...] = jnp.where(rank < topn, jnp.where(jidx <= blk_t, 0.0, NEG), NEG)

    row_a = lax.broadcasted_iota(I32, (ta, Q_BLOCK), 0)
    n_ta = (q0 + Q_BLOCK + ta - 1) // ta

    def slc_bias(k0):
        j0 = k0 // SLC_BLOCK
        rows = [jnp.broadcast_to(sel_ref[pl.ds(j0 + jj, 1), :], (SLC_BLOCK, Q_BLOCK))
                for jj in range(ta // SLC_BLOCK)]
        return jnp.where(k0 + row_a <= t_row, jnp.concatenate(rows, axis=0), NEG)

    o_slc = _masked_attention(
        qT,
        lambda k0: knat_ref[0, pl.ds(k0, ta), HEAD_DIM:2 * HEAD_DIM],
        lambda k0: vT_ref[0, HEAD_DIM:2 * HEAD_DIM, pl.ds(k0, ta)],
        slc_bias, 0, n_ta, ta, s_ref, p_ref, acc_ref)

    def win_bias(k0):
        pos = k0 + row_a
        return jnp.where(pos <= t_row, jnp.where(t_row - pos < WINDOW, 0.0, NEG), NEG)

    o_win = _masked_attention(
        qT,
        lambda k0: knat_ref[0, pl.ds(k0, ta), 2 * HEAD_DIM:3 * HEAD_DIM],
        lambda k0: vT_ref[0, 2 * HEAD_DIM:3 * HEAD_DIM, pl.ds(k0, ta)],
        win_bias, jnp.maximum((q0 - WINDOW) // ta, 0), n_ta, ta, sw_ref, pw_ref, accw_ref)

    g = 1.0 / (1.0 + jnp.exp(-smallT_ref[0, IDX_HEADS:IDX_HEADS + 3 * N_HEADS, :]))
    parts = []
    for hh in range(N_HEADS):
        cols = slice(Q_BLOCK * hh, Q_BLOCK * (hh + 1))
        parts.append(g[3 * hh:3 * hh + 1, :] * o_cmp[:, cols]
                     + g[3 * hh + 1:3 * hh + 2, :] * o_slc[:, cols]
                     + g[3 * hh + 2:3 * hh + 3, :] * o_win[:, cols])
    o_tok = jnp.concatenate(parts, axis=0).T
    out_ref[0] = (o_tok * gb_ref[0]).astype(BF16)


def _nsa_call(qbT, smallT, gates, knat, vT, kcmp, vcmpT, ovT, *, ta):
    B, _, T = qbT.shape
    ncp, nslc = T // CMP_STRIDE, T // SLC_BLOCK
    qblk = lambda r: pl.BlockSpec((1, r, Q_BLOCK), lambda b, i: (b, 0, i))
    return pl.pallas_call(
        functools.partial(_nsa_kernel, ta=ta, seq=T),
        grid=(B, T // Q_BLOCK),
        in_specs=[qblk(256), qblk(64),
                  pl.BlockSpec((1, Q_BLOCK, 256), lambda b, i: (b, i, 1)),
                  pl.BlockSpec((1, T, 256), lambda b, i: (b, 0, 0)),
                  pl.BlockSpec((1, 192, T), lambda b, i: (b, 0, 0)),
                  pl.BlockSpec((1, ncp, HEAD_DIM), lambda b, i: (b, 0, 0)),
                  pl.BlockSpec((1, HEAD_DIM, ncp), lambda b, i: (b, 0, 0)),
                  pl.BlockSpec((nslc, ncp), lambda b, i: (0, 0))],
        out_specs=pl.BlockSpec((1, Q_BLOCK, 256), lambda b, i: (b, i, 0)),
        out_shape=jax.ShapeDtypeStruct((B, T, 256), BF16),
        scratch_shapes=[pltpu.VMEM((nslc, Q_BLOCK), F32)] + _attention_scratch(ta)
        + _attention_scratch(ta),
        compiler_params=pltpu.CompilerParams(
            dimension_semantics=("parallel", "arbitrary"), vmem_limit_bytes=VMEM_LIMIT),
        name="nsa",
    )(qbT, smallT, gates, knat, vT, kcmp, vcmpT, ovT)


def _out_kernel(x_ref, ma_ref, mb_ref, mcd_ref, wo_ref, fw_ref, o_ref):
    y = (x_ref[...] + _dot(ma_ref[...], wo_ref[0:256, :]) + _dot(mb_ref[...], wo_ref[256:512, :])
         + _dot(mcd_ref[...], wo_ref[512:1024, :]))
    ms = jnp.mean(y * y, axis=-1, keepdims=True)
    o_ref[...] = y * lax.rsqrt(ms + EPS) * fw_ref[...]


def _out_call(x2, ma, mb, mcd, wo, fw, *, tm):
    R, D = x2.shape
    rows = lambda w: pl.BlockSpec((tm, w), lambda r: (r, 0))
    return pl.pallas_call(
        _out_kernel,
        grid=(R // tm,),
        in_specs=[rows(D), rows(256), rows(256), rows(512),
                  pl.BlockSpec((D, D), lambda r: (0, 0)), pl.BlockSpec((1, D), lambda r: (0, 0))],
        out_specs=rows(D),
        out_shape=jax.ShapeDtypeStruct((R, D), F32),
        compiler_params=pltpu.CompilerParams(
            dimension_semantics=("parallel",), vmem_limit_bytes=VMEM_LIMIT),
        name="outproj",
    )(x2, ma, mb, mcd, wo, fw)


def _rope_angles(pos, rot):
    half = rot // 2
    inv = ROPE_THETA ** (-jnp.arange(half, dtype=F32) / half)
    ang = pos.astype(F32)[:, None] * inv[None, :]
    return jnp.cos(ang), jnp.sin(ang)


def _rope_tables(T):
    pos = jnp.arange(T)
    c16, s16 = _rope_angles(pos, ROT_DIM)
    c8, s8 = _rope_angles(pos, IDX_ROT)
    cat = lambda parts: jnp.concatenate(parts, axis=1)
    z = lambda w, n=T: jnp.zeros((n, w), F32)
    o = lambda w, n=T: jnp.ones((n, w), F32)
    C = cat([c16, c16, o(48)] * 3 + [c8, c8, o(56)])
    Sa = cat([-s16, z(56)] * 3 + [z(64)])
    Sb = cat([z(8), s16, z(48)] * 3 + [z(64)])
    Sc = cat([z(KI_OFF), -s8, z(60)])
    Sd = cat([z(KI_OFF + 4), s8, z(56)])
    tabn = jnp.stack([C, Sa, Sb, Sc, Sd])
    tabt = jnp.stack([c16.T, s16.T, jnp.concatenate([c8.T, c8.T]),
                      jnp.concatenate([-s8.T, s8.T])])
    ncp = T // CMP_STRIDE
    cc, sc_ = _rope_angles(jnp.arange(ncp) * CMP_STRIDE + CMP_LEN - 1, ROT_DIM)
    tabc = jnp.stack([cat([cc, cc, o(112, ncp)]), cat([-sc_, z(120, ncp)]),
                      cat([z(8, ncp), sc_, z(112, ncp)])])
    return tabn, tabt, tabc


def _split_offsets():
    widths = (("a_q", 256), ("a_k", 64), ("a_v", 64), ("a_qi", 256), ("a_ki", 32), ("a_wi", 8),
              ("a_gate", 256), ("b_q", 256), ("b_kc", 64), ("b_vc", 64), ("b_ks", 64),
              ("b_vs", 64), ("b_kw", 64), ("b_vw", 64), ("b_g", 12), ("b_gate", 256),
              ("c_b", 256), ("c_c", 256), ("c_x", 256), ("c_gate", 256), ("d_u", 256),
              ("d_gate", 256))
    offs, o = {}, 0
    for name, w in widths:
        offs[name] = (o, o + w)
        o += w
    return offs


def _relaid_weights(w_in, pool_w, pe_cmp, w_cmp_k, w_cmp_v):
    offs = _split_offsets()
    col = lambda n: w_in[:, :, offs[n][0]:offs[n][1]]
    L, D, _ = w_in.shape
    wn = jnp.concatenate(
        [col("a_k"), col("b_ks"), col("b_kw"), col("a_ki"), jnp.zeros((L, D, 32), F32),
         col("b_kc"), col("b_vc"), col("a_gate"), col("b_gate"), col("c_b"), col("c_c"),
         col("c_x"), col("c_gate"), col("d_u"), col("d_gate")], axis=2).astype(BF16)
    qscale = HEAD_DIM ** -0.5 * LOG2E
    iscale = (IDX_DIM ** -0.5) * (IDX_HEADS ** -0.5)
    wt = jnp.swapaxes(jnp.concatenate(
        [col("a_q") * qscale, col("a_qi"), col("b_q") * qscale, col("a_v"), col("b_vs"),
         col("b_vw"), col("a_wi") * iscale, col("b_g"), jnp.zeros((L, D, 44), F32)],
        axis=2), 1, 2).astype(BF16)
    groups, pch = pool_w.shape[1], pool_w.shape[2]
    poolw = jnp.concatenate(
        [jnp.concatenate([jnp.zeros((L, pch, g * pch), F32), pool_w[:, g],
                          jnp.zeros((L, pch, (groups - 1 - g) * pch), F32)], axis=2)
         for g in range(groups)], axis=1).astype(BF16)
    half = CMP_LEN // 2
    wk4 = w_cmp_k.reshape(L, CMP_LEN, HEAD_DIM, HEAD_DIM)
    wv4 = w_cmp_v.reshape(L, CMP_LEN, HEAD_DIM, HEAD_DIM)
    zk = jnp.zeros((L, half, HEAD_DIM, HEAD_DIM), F32)

    def chunk_w(lo):
        top = jnp.concatenate([wk4[:, lo:lo + half], zk], axis=3)
        bot = jnp.concatenate([zk, wv4[:, lo:lo + half]], axis=3)
        return jnp.concatenate([top, bot], axis=2).reshape(L, half * 128, 128)

    wc = jnp.stack([chunk_w(0), chunk_w(half)], axis=1).astype(BF16)
    pe_rows = lambda lo: jnp.concatenate([pe_cmp[:, lo:lo + half]] * 2, axis=2).reshape(L, 1, half * 128)
    pe2 = jnp.stack([pe_rows(0), pe_rows(half)], axis=1)
    return wn, wt, poolw, wc, pe2


def _overlap_T(T):
    ncp, nslc = T // CMP_STRIDE, T // SLC_BLOCK
    n_cmp = (T - CMP_LEN) // CMP_STRIDE + 1
    cs = np.arange(ncp) * CMP_STRIDE
    ce = cs + CMP_LEN - 1
    ss = np.arange(nslc) * SLC_BLOCK
    ov = (cs[None, :] < ss[:, None] + SLC_BLOCK) & (ce[None, :] >= ss[:, None]) & (np.arange(ncp)[None, :] < n_cmp)
    return jnp.asarray(ov.astype(np.float32))


def kernel(x, norm_w, w_in, w_out, conv_w, pe_cmp, w_cmp_k, w_cmp_v, pool_w, pool_scale, final_norm_w):
    B, T, D = x.shape
    depth = w_in.shape[0]
    tm = min(512, T)
    sub = min(256, tm)
    tk = min(256, T)
    ta = min(128, T)
    tabn, tabt, tabc = _rope_tables(T)
    ovT = _overlap_T(T)
    fw = final_norm_w.reshape(1, D)
    wn, wt, poolw, wc, pe2 = _relaid_weights(w_in, pool_w, pe_cmp, w_cmp_k, w_cmp_v)
    wo = w_out.astype(BF16)
    prev = None
    for l in range(depth):
        outs = _proj_call(x, prev, norm_w[l].reshape(1, D), wn[l], wt[l], tabn, tabt, conv_w[l], poolw[l],
                          pool_scale[l].reshape(1, W_GROUP), tm=tm, sub=sub)
        if prev is not None:
            x, outs = outs[0], outs[1:]
        knat, kvc, gates, mcd, qaT, qiT, qbT, vT, smallT = outs
        kcmp, vcmpT = _cmp_call(kvc.reshape(B, T // CMP_STRIDE, CMP_STRIDE * 128), pe2[l], wc[l], tabc)
        ma = _dsa_call(qaT, qiT, smallT, gates, knat, vT, tk=tk, ta=ta)
        mb = _nsa_call(qbT, smallT, gates, knat, vT, kcmp, vcmpT, ovT, ta=ta)
        prev = (ma, mb, mcd, wo[l])
    x = _out_call(x.reshape(B * T, D), ma.reshape(B * T, 256), mb.reshape(B * T, 256),
                  mcd.reshape(B * T, 512), wo[depth - 1], fw, tm=tm).reshape(B, T, D)
    return x
```
